```python
import jax, jax.numpy as jnp
from jax import lax
import numpy as np

D_MODEL = 1024
BATCH = 8
SEQ = 8192
DEPTH = 1

ATTN_GROUPS = ((128, 1), (512, 4), (2048, 16))
N_ATTN_GROUPS = 3
HEADS_PER_GROUP = 8
HEAD_DIM_A = 64
ATTN_GROUP_WIDTH = HEADS_PER_GROUP * HEAD_DIM_A
ATTN_QKV_WIDTH = N_ATTN_GROUPS * ATTN_GROUP_WIDTH
DN_HEADS = 8
DN_HEAD_DIM = 128
DN_WIDTH = DN_HEADS * DN_HEAD_DIM
CONV_WIDTH = 4
CHUNK = 64
N_EXPERT_GROUPS = 4
EXPERTS_PER_GROUP = 4
N_EXPERTS = N_EXPERT_GROUPS * EXPERTS_PER_GROUP
TOP_K = 2
D_EXPERT = 256
PLE_DIM = 256
EPS = 1e-6
IN_WIDTHS = (3 * ATTN_QKV_WIDTH, 3 * DN_WIDTH, DN_WIDTH, DN_HEADS, DN_HEADS, D_MODEL, D_MODEL)
IN_TOTAL = 3 * ATTN_QKV_WIDTH + 4 * DN_WIDTH + 2 * DN_HEADS + 2 * D_MODEL

kernel_name = "hybrid_dilated_attn_gated_deltanet_hier_moe"


def rms_norm(x, gain):
    xf = x.astype(jnp.float32)
    y = xf * lax.rsqrt(jnp.mean(xf * xf, axis=-1, keepdims=True) + EPS)
    return y * gain.astype(jnp.float32)


def l2_norm(x):
    return x * lax.rsqrt(jnp.sum(x * x, axis=-1, keepdims=True) + EPS)


def split_columns(t, widths):
    out, start = [], 0
    for w in widths:
        out.append(t[..., start:start + w])
        start += w
    return out


def dilated_window_attention(q, k, v, window, dilation):
    b, s, h, dh = q.shape
    blk = window // dilation
    span = blk * dilation
    s_pad = -(-s // span) * span
    n_sub = s_pad // dilation
    nblk = n_sub // blk

    def to_blocks(t):
        t = jnp.pad(t, ((0, 0), (0, s_pad - s), (0, 0), (0, 0)))
        t = t.reshape(b, n_sub, dilation, h, dh).transpose(0, 2, 3, 1, 4)
        return t.reshape(b, dilation, h, nblk, blk, dh)

    def with_prev(t):
        prev = jnp.pad(t[:, :, :, :-1], ((0, 0), (0, 0), (0, 0), (1, 0), (0, 0), (0, 0)))
        return jnp.concatenate([prev, t], axis=4)

    qb = to_blocks(q)
    kb = with_prev(to_blocks(k))
    vb = with_prev(to_blocks(v))
    scores = jnp.einsum('brhnqd,brhnkd->brhnqk', qb, kb) * (dh ** -0.5)
    qi = jnp.arange(blk)[:, None]
    kj = jnp.arange(2 * blk)[None, :]
    dist = blk + qi - kj
    band = (dist >= 0) & (dist <= blk)
    has_prev = jnp.arange(nblk)[:, None, None] > 0
    valid = band[None] & (has_prev | (kj >= blk)[None])
    scores = jnp.where(valid, scores, -jnp.inf)
    m = jnp.max(scores, axis=-1, keepdims=True)
    e = jnp.exp(scores - m)
    denom = jnp.sum(e, axis=-1)
    o = jnp.einsum('brhnqk,brhnkd->brhnqd', e, vb) / denom[..., None]
    lse = m[..., 0] + jnp.log(denom)
    o = o.reshape(b, dilation, h, n_sub, dh).transpose(0, 3, 1, 2, 4).reshape(b, s_pad, h, dh)[:, :s]
    lse = lse.reshape(b, dilation, h, n_sub).transpose(0, 3, 1, 2).reshape(b, s_pad, h)[:, :s]
    return o, lse


def causal_short_conv(x, w):
    kw = w.shape[0]
    s = x.shape[1]
    xp = jnp.pad(x, ((0, 0), (kw - 1, 0), (0, 0)))
    y = xp[:, kw - 1:kw - 1 + s] * w[kw - 1]
    for j in range(kw - 1):
        y = y + xp[:, j:j + s] * w[j]
    return jax.nn.silu(y)


def gated_delta_rule(q, k, v, beta, g):
    b, s, h, dk = q.shape
    dv = v.shape[-1]
    n = s // CHUNK

    def chunkify(t):
        return t.reshape(b, n, CHUNK, h, -1).transpose(1, 0, 3, 2, 4)

    qc, kc, vc = chunkify(q), chunkify(k), chunkify(v)
    bc = beta.reshape(b, n, CHUNK, h).transpose(1, 0, 3, 2)
    gcum = jnp.cumsum(g.reshape(b, n, CHUNK, h).transpose(1, 0, 3, 2), axis=-1)
    tril = jnp.tril(jnp.ones((CHUNK, CHUNK), dtype=bool))
    tril_strict = jnp.tril(jnp.ones((CHUNK, CHUNK), dtype=bool), -1)
    decay_mat = jnp.exp(jnp.where(tril, gcum[..., :, None] - gcum[..., None, :], -jnp.inf))
    kbeta = kc * bc[..., None]
    a_mat = jnp.where(tril_strict, jnp.einsum('nbhid,nbhjd->nbhij', kbeta, kc) * decay_mat, 0.0)
    eye = jnp.eye(CHUNK, dtype=jnp.float32)
    t_mat = lax.linalg.triangular_solve(eye + a_mat, jnp.broadcast_to(eye, a_mat.shape),
                                        left_side=True, lower=True, unit_diagonal=True)
    u = jnp.einsum('nbhij,nbhjd->nbhid', t_mat, vc * bc[..., None])
    w = jnp.einsum('nbhij,nbhjd->nbhid', t_mat, kbeta * jnp.exp(gcum)[..., None])
    attn_intra = jnp.where(tril, jnp.einsum('nbhid,nbhjd->nbhij', qc, kc) * decay_mat, 0.0)
    g_last = gcum[..., -1]
    q_dec = qc * jnp.exp(gcum)[..., None]
    k_tail = kc * jnp.exp(g_last[..., None] - gcum)[..., None]

    def step(state, xs):
        qd, ui, wi, ai, gl, kt = xs
        v_new = ui - jnp.einsum('bhcd,bhde->bhce', wi, state)
        o = jnp.einsum('bhcd,bhde->bhce', qd, state) + jnp.einsum('bhij,bhje->bhie', ai, v_new)
        state = state * jnp.exp(gl)[..., None, None] + jnp.einsum('bhcd,bhce->bhde', kt, v_new)
        return state, o

    s0 = jnp.zeros((b, h, dk, dv), jnp.float32)
    _, o = lax.scan(step, s0, (q_dec, u, w, attn_intra, g_last, k_tail))
    return o.transpose(1, 0, 3, 2, 4).reshape(b, s, h, dv)


def hierarchical_moe(hn, w_rg, b_rg, w_re, b_re, w_gate, w_up, w_down):
    bsz, s, d = hn.shape
    t = hn.reshape(-1, d)
    p_group = jax.nn.softmax(t @ w_rg.astype(jnp.float32) + b_rg.astype(jnp.float32), axis=-1)
    g_idx = jnp.argmax(p_group, axis=-1)
    p_g = jnp.take_along_axis(p_group, g_idx[:, None], axis=-1)
    e_logits = (t @ w_re.astype(jnp.float32) + b_re.astype(jnp.float32)).reshape(-1, N_EXPERT_GROUPS, EXPERTS_PER_GROUP)
    sel_logits = jnp.take_along_axis(e_logits, g_idx[:, None, None], axis=1)[:, 0]
    p_exp = jax.nn.softmax(sel_logits, axis=-1)
    top_p, top_i = lax.top_k(p_exp, TOP_K)
    top_p = top_p / jnp.sum(top_p, axis=-1, keepdims=True)
    w_in_group = jnp.sum(jax.nn.one_hot(top_i, EXPERTS_PER_GROUP) * top_p[..., None], axis=1)
    combine = jax.nn.one_hot(g_idx, N_EXPERT_GROUPS)[:, :, None] * (p_g * w_in_group)[:, None, :]
    y = jnp.zeros_like(t)
    for gi in range(N_EXPERT_GROUPS):
        hg = jnp.einsum('td,edf->tef', t, w_gate[gi])
        hu = jnp.einsum('td,edf->tef', t, w_up[gi])
        act = jax.nn.silu(hg) * hu * combine[:, gi, :, None]
        y = y + jnp.einsum('tef,efd->td', act, w_down[gi])
    return y.reshape(bsz, s, d)


def setup_inputs(seed: int = 0) -> dict:
    key = jax.random.key(seed)
    ks = jax.random.split(key, 24)
    f32 = jnp.float32

    def nrm(k, shape, scale):
        return jax.random.normal(k, shape, f32) * scale

    def gain(k, n):
        return 1.0 + 0.05 * jax.random.normal(k, (DEPTH, n), f32)

    a_init = jax.random.uniform(ks[6], (DEPTH, DN_HEADS), f32, 1.0, 16.0)
    dt = jnp.exp(jax.random.uniform(ks[7], (DEPTH, DN_HEADS), f32, float(np.log(1e-3)), float(np.log(1e-1))))
    dt_bias = dt + jnp.log(-jnp.expm1(-dt))
    return {
        "x": nrm(ks[0], (BATCH, SEQ, D_MODEL), 1.0),
        "p": nrm(ks[1], (DEPTH, BATCH, SEQ, PLE_DIM), 1.0),
        "norm_mix": gain(ks[2], D_MODEL),
        "w_in": nrm(ks[3], (DEPTH, D_MODEL, IN_TOTAL), D_MODEL ** -0.5),
        "q_norm": gain(ks[4], HEAD_DIM_A),
        "k_norm": gain(ks[5], HEAD_DIM_A),
        "conv_w": nrm(ks[8], (DEPTH, CONV_WIDTH, 3 * DN_WIDTH), CONV_WIDTH ** -0.5),
        "a_log": jnp.log(a_init),
        "dt_bias": dt_bias,
        "dn_out_norm": gain(ks[9], DN_HEAD_DIM),
        "w_branch_a": nrm(ks[10], (DEPTH, ATTN_GROUP_WIDTH, D_MODEL), ATTN_GROUP_WIDTH ** -0.5),
        "w_branch_b": nrm(ks[11], (DEPTH, DN_WIDTH, D_MODEL), DN_WIDTH ** -0.5),
        "w_out": nrm(ks[12], (DEPTH, D_MODEL, D_MODEL), D_MODEL ** -0.5),
        "norm_ffn": gain(ks[13], D_MODEL),
        "w_router_group": nrm(ks[14], (DEPTH, D_MODEL, N_EXPERT_GROUPS), D_MODEL ** -0.5),
        "b_router_group": nrm(ks[15], (DEPTH, N_EXPERT_GROUPS), 0.01),
        "w_router_expert": nrm(ks[16], (DEPTH, D_MODEL, N_EXPERTS), D_MODEL ** -0.5),
        "b_router_expert": nrm(ks[17], (DEPTH, N_EXPERTS), 0.01),
        "w_expert_gate": nrm(ks[18], (DEPTH, N_EXPERT_GROUPS, EXPERTS_PER_GROUP, D_MODEL, D_EXPERT), D_MODEL ** -0.5),
        "w_expert_up": nrm(ks[19], (DEPTH, N_EXPERT_GROUPS, EXPERTS_PER_GROUP, D_MODEL, D_EXPERT), D_MODEL ** -0.5),
        "w_expert_down": nrm(ks[20], (DEPTH, N_EXPERT_GROUPS, EXPERTS_PER_GROUP, D_EXPERT, D_MODEL), D_EXPERT ** -0.5),
        "norm_ple": gain(ks[21], D_MODEL),
        "w_ple": nrm(ks[22], (DEPTH, PLE_DIM, D_MODEL), PLE_DIM ** -0.5),
        "w_ple_gate": nrm(ks[23], (DEPTH, D_MODEL, D_MODEL), D_MODEL ** -0.5),
    }


def reference(x, p, norm_mix, w_in, q_norm, k_norm, conv_w, a_log, dt_bias, dn_out_norm,
              w_branch_a, w_branch_b, w_out, norm_ffn, w_router_group, b_router_group,
              w_router_expert, b_router_expert, w_expert_gate, w_expert_up, w_expert_down,
              norm_ple, w_ple, w_ple_gate):
    b, s, d = x.shape
    x = x.astype(jnp.float32)
    for i in range(DEPTH):
        h = rms_norm(x, norm_mix[i])
        proj = h @ w_in[i].astype(jnp.float32)
        qkv_a, qkv_b, z_b, beta_b, alpha_b, gate_a, gate_b = split_columns(proj, IN_WIDTHS)

        qkv_a = qkv_a.reshape(b, s, 3, N_ATTN_GROUPS, HEADS_PER_GROUP, HEAD_DIM_A)
        qa = rms_norm(qkv_a[:, :, 0], q_norm[i])
        ka = rms_norm(qkv_a[:, :, 1], k_norm[i])
        va = qkv_a[:, :, 2]
        outs, lses = [], []
        for gi, (win, dil) in enumerate(ATTN_GROUPS):
            o_g, lse_g = dilated_window_attention(qa[:, :, gi], ka[:, :, gi], va[:, :, gi], win, dil)
            outs.append(o_g)
            lses.append(lse_g)
        mix_w = jax.nn.softmax(jnp.stack(lses, axis=0), axis=0)
        o_a = jnp.sum(mix_w[..., None] * jnp.stack(outs, axis=0), axis=0).reshape(b, s, ATTN_GROUP_WIDTH)

        qkv_b = causal_short_conv(qkv_b, conv_w[i].astype(jnp.float32))
        qb, kb, vb = split_columns(qkv_b, (DN_WIDTH, DN_WIDTH, DN_WIDTH))
        qb = l2_norm(qb.reshape(b, s, DN_HEADS, DN_HEAD_DIM)) * (DN_HEAD_DIM ** -0.5)
        kb = l2_norm(kb.reshape(b, s, DN_HEADS, DN_HEAD_DIM))
        vb = vb.reshape(b, s, DN_HEADS, DN_HEAD_DIM)
        beta = jax.nn.sigmoid(beta_b)
        g = -jnp.exp(a_log[i].astype(jnp.float32)) * jax.nn.softplus(alpha_b + dt_bias[i].astype(jnp.float32))
        o_b = gated_delta_rule(qb, kb, vb, beta, g)
        o_b = rms_norm(o_b, dn_out_norm[i]) * jax.nn.silu(z_b.reshape(b, s, DN_HEADS, DN_HEAD_DIM))
        o_b = o_b.reshape(b, s, DN_WIDTH)

        merged = (jax.nn.sigmoid(gate_a) * (o_a @ w_branch_a[i].astype(jnp.float32))
                  + jax.nn.sigmoid(gate_b) * (o_b @ w_branch_b[i].astype(jnp.float32)))
        x = x + merged @ w_out[i].astype(jnp.float32)

        h2 = rms_norm(x, norm_ffn[i])
        x = x + hierarchical_moe(h2, w_router_group[i], b_router_group[i], w_router_expert[i],
                                 b_router_expert[i], w_expert_gate[i].astype(jnp.float32),
                                 w_expert_up[i].astype(jnp.float32), w_expert_down[i].astype(jnp.float32))

        h3 = rms_norm(x, norm_ple[i])
        ple = p[i].astype(jnp.float32) @ w_ple[i].astype(jnp.float32)
        x = x + jax.nn.sigmoid(h3 @ w_ple_gate[i].astype(jnp.float32)) * ple
    return x
```

```python
import functools

import jax
import jax.numpy as jnp
from jax import lax
from jax.experimental import pallas as pl
from jax.experimental.pallas import tpu as pltpu

F32 = jnp.float32
BF16 = jnp.bfloat16
EPS = 1e-6

ATTN_GROUPS = ((128, 1), (512, 4), (2048, 16))
N_GROUPS = 3
HEADS_A = 8
HEAD_DIM_A = 64
GROUP_WIDTH = HEADS_A * HEAD_DIM_A
ATTN_BLOCK = 128
DN_HEADS = 8
DN_DIM = 128
DN_WIDTH = DN_HEADS * DN_DIM
CONV_WIDTH = 4
CHUNK = 64
N_EXPERT_GROUPS = 4
EXPERTS_PER_GROUP = 4
D_EXPERT = 256
LANES = 128

PROJ_TILE = 1536
COL_Q, COL_K, COL_B, COL_Z, COL_GA, COL_GB, COL_V = 0, 1536, 3072, 6144, 7168, 8192, 9216
PROJ_WIDTH = 10752

VMEM_LIMIT = 56 * 1024 * 1024


def _dot(a, b):
    return jnp.dot(a, b, preferred_element_type=F32)


def _dot_nt(a, b):
    return lax.dot_general(a, b, (((1,), (1,)), ((), ())), preferred_element_type=F32)


def _dot_tn(a, b):
    return lax.dot_general(a, b, (((0,), (0,)), ((), ())), preferred_element_type=F32)


def _split2(v):
    hi = v.astype(BF16)
    lo = (v - hi.astype(F32)).astype(BF16)
    return hi, lo


def _split3(v):
    a = v.astype(BF16)
    r = v - a.astype(F32)
    b = r.astype(BF16)
    c = (r - b.astype(F32)).astype(BF16)
    return a, b, c


def _sigmoid(v):
    return 1.0 / (1.0 + jnp.exp(-v))


def _silu(v):
    return v / (1.0 + jnp.exp(-v))


def _params(sem):
    return pltpu.CompilerParams(dimension_semantics=sem, vmem_limit_bytes=VMEM_LIMIT)


def _inproj_kernel(x_ref, gain_ref, w_ref, wab_hi_ref, wab_lo_ref, qkgain_ref, red_ref, exp_ref,
                   proj_ref, ab_ref, h_scr):
    j = pl.program_id(1)

    @pl.when(j == 0)
    def _():
        x = x_ref[...]
        ms = jnp.mean(x * x, axis=-1, keepdims=True)
        h = x * lax.rsqrt(ms + EPS) * gain_ref[...]
        hi, lo = _split2(h)
        h_scr[...] = hi
        ab_ref[...] = _dot(hi, wab_hi_ref[...]) + _dot(lo, wab_hi_ref[...]) + _dot(hi, wab_lo_ref[...])

    acc = _dot(h_scr[...], w_ref[...])

    @pl.when(j < 2)
    def _():
        ss = _dot((acc * acc).astype(BF16), red_ref[...])
        r = lax.rsqrt(ss * (1.0 / HEAD_DIM_A) + EPS)
        r_hi, r_lo = _split2(r)
        rexp = _dot(r_hi, exp_ref[...]) + _dot(r_lo, exp_ref[...])
        proj_ref[...] = (acc * rexp * qkgain_ref[...]).astype(BF16)

    @pl.when(j >= 2)
    def _():
        proj_ref[...] = acc.astype(BF16)


def _in_projection(x2, norm_gain, w_main, wab_hi, wab_lo, qkgain, tm):
    t, d = x2.shape
    n_j = PROJ_WIDTH // PROJ_TILE
    heads_per_tile = PROJ_TILE // HEAD_DIM_A
    col = jnp.arange(PROJ_TILE) // HEAD_DIM_A
    red = (col[:, None] == jnp.arange(LANES)[None, :]).astype(BF16)
    expm = red.T
    assert heads_per_tile <= LANES
    return pl.pallas_call(
        _inproj_kernel,
        grid=(t // tm, n_j),
        in_specs=[
            pl.BlockSpec((tm, d), lambda i, j: (i, 0)),
            pl.BlockSpec((1, d), lambda i, j: (0, 0)),
            pl.BlockSpec((d, PROJ_TILE), lambda i, j: (0, j)),
            pl.BlockSpec((d, LANES), lambda i, j: (0, 0)),
            pl.BlockSpec((d, LANES), lambda i, j: (0, 0)),
            pl.BlockSpec((1, PROJ_TILE), lambda i, j: (0, jnp.minimum(j, 1))),
            pl.BlockSpec((PROJ_TILE, LANES), lambda i, j: (0, 0)),
            pl.BlockSpec((LANES, PROJ_TILE), lambda i, j: (0, 0)),
        ],
        out_specs=[
            pl.BlockSpec((tm, PROJ_TILE), lambda i, j: (i, j)),
            pl.BlockSpec((tm, LANES), lambda i, j: (i, 0)),
        ],
        out_shape=[
            jax.ShapeDtypeStruct((t, PROJ_WIDTH), BF16),
            jax.ShapeDtypeStruct((t, LANES), F32),
        ],
        scratch_shapes=[pltpu.VMEM((tm, d), BF16)],
        compiler_params=_params(("parallel", "arbitrary")),
        name="in_proj",
    )(x2, norm_gain, w_main, wab_hi, wab_lo, qkgain, red, expm)


def _attn_kernel(q_ref, kc_ref, kp_ref, vc_ref, vp_ref, o_ref, lse_ref, *, n_blk):
    n = pl.program_id(2)
    blk = ATTN_BLOCK
    qi = lax.broadcasted_iota(jnp.int32, (blk, blk), 0)
    kj = lax.broadcasted_iota(jnp.int32, (blk, blk), 1)
    cur_ok = kj <= qi
    prev_band = kj >= qi
    lane = lax.broadcasted_iota(jnp.int32, (blk, LANES), 1)
    low_half = lane < HEAD_DIM_A
    neg_inf = jnp.float32(-jnp.inf)

    for b in range(n_blk):
        rows = slice(b * blk, (b + 1) * blk)
        if b == 0:
            prev_ok = jnp.logical_and(prev_band, n > 0)
        else:
            prev_ok = prev_band
        lse_tile = jnp.zeros((blk, LANES), F32)
        for pair in range(HEADS_A // 2):
            cols = slice(pair * LANES, (pair + 1) * LANES)
            q2 = q_ref[rows, cols]
            kc2 = kc_ref[rows, cols]
            vc2 = vc_ref[rows, cols]
            if b == 0:
                kp2 = kp_ref[:, cols]
                vp2 = vp_ref[:, cols]
            else:
                prow = slice((b - 1) * blk, b * blk)
                kp2 = kc_ref[prow, cols]
                vp2 = vc_ref[prow, cols]
            outs = []
            for half in range(2):
                keep = low_half if half == 0 else jnp.logical_not(low_half)
                zero = jnp.zeros_like(kc2)
                s_c = _dot_nt(q2, jnp.where(keep, kc2, zero))
                s_p = _dot_nt(q2, jnp.where(keep, kp2, zero))
                s_c = jnp.where(cur_ok, s_c, neg_inf)
                s_p = jnp.where(prev_ok, s_p, neg_inf)
                m = jnp.maximum(jnp.max(s_c, axis=-1, keepdims=True), jnp.max(s_p, axis=-1, keepdims=True))
                e_c = jnp.exp(s_c - m)
                e_p = jnp.exp(s_p - m)
                denom = jnp.sum(e_c, axis=-1, keepdims=True) + jnp.sum(e_p, axis=-1, keepdims=True)
                num = _dot(e_c.astype(BF16), vc2) + _dot(e_p.astype(BF16), vp2)
                outs.append(num / denom)
                head = 2 * pair + half
                lse_tile = jnp.where(lane == head, m + jnp.log(denom), lse_tile)
            o_ref[rows, cols] = jnp.where(low_half, outs[0], outs[1]).astype(BF16)
        lse_ref[rows, :] = lse_tile


def _attention_group(proj3, g, dilation, tq):
    b, s, w = proj3.shape
    n_sub = s // dilation
    tq = min(tq, n_sub)
    n_blk = tq // ATTN_BLOCK
    view = proj3.reshape(b, n_sub, dilation * w)
    cb = w // GROUP_WIDTH
    q0, k0, v0 = COL_Q // GROUP_WIDTH + g, COL_K // GROUP_WIDTH + g, COL_V // GROUP_WIDTH + g

    def cur(c0):
        return pl.BlockSpec((None, tq, GROUP_WIDTH), lambda bi, r, n: (bi, n, r * cb + c0))

    def prev(c0):
        return pl.BlockSpec((None, ATTN_BLOCK, GROUP_WIDTH),
                            lambda bi, r, n: (bi, jnp.maximum(n * n_blk - 1, 0), r * cb + c0))

    o, lse = pl.pallas_call(
        functools.partial(_attn_kernel, n_blk=n_blk),
        grid=(b, dilation, n_sub // tq),
        in_specs=[cur(q0), cur(k0), prev(k0), cur(v0), prev(v0)],
        out_specs=[
            pl.BlockSpec((None, tq, GROUP_WIDTH), lambda bi, r, n: (bi, n, r)),
            pl.BlockSpec((None, tq, LANES), lambda bi, r, n: (bi, n, r)),
        ],
        out_shape=[
            jax.ShapeDtypeStruct((b, n_sub, dilation * GROUP_WIDTH), BF16),
            jax.ShapeDtypeStruct((b, n_sub, dilation * LANES), F32),
        ],
        compiler_params=_params(("parallel", "parallel", "parallel")),
        name=f"attn_d{dilation}",
    )(view, view, view, view, view)
    return o.reshape(b * s, GROUP_WIDTH), lse.reshape(b * s, LANES)


def _deltanet_kernel(xb_ref, z_ref, ab_ref, convw_ref, alog_ref, dtb_ref, ognorm_ref, lcum_ref,
                     o_ref, qkv_scr, gc_scr, beta_scr, carry_scr, state_scr, *, tile):
    t_idx = pl.program_id(1)
    n_chunks = tile // CHUNK
    halo = 8

    @pl.when(t_idx == 0)
    def _():
        carry_scr[...] = jnp.zeros_like(carry_scr)
        state_scr[...] = jnp.zeros_like(state_scr)

    for c in range(3 * DN_HEADS):
        cols = slice(c * DN_DIM, (c + 1) * DN_DIM)
        cur = xb_ref[:, cols].astype(F32)
        ext = jnp.concatenate([carry_scr[:, cols], cur], axis=0)
        y = ext[halo:halo + tile] * convw_ref[CONV_WIDTH - 1:CONV_WIDTH, cols]
        for j in range(CONV_WIDTH - 1):
            off = halo - (CONV_WIDTH - 1) + j
            y = y + ext[off:off + tile] * convw_ref[j:j + 1, cols]
        carry_scr[:, cols] = cur[tile - halo:tile]
        y = _silu(y)
        if c < 2 * DN_HEADS:
            y = y * lax.rsqrt(jnp.sum(y * y, axis=-1, keepdims=True) + EPS)
            if c < DN_HEADS:
                y = y * (DN_DIM ** -0.5)
        qkv_scr[:, cols] = y

    ab = ab_ref[...]
    beta_scr[...] = _sigmoid(ab)
    sp_in = ab + dtb_ref[...]
    softplus = jnp.maximum(sp_in, 0.0) + jnp.log(1.0 + jnp.exp(-jnp.abs(sp_in)))
    gdec = -jnp.exp(alog_ref[...]) * softplus
    g1, g2, g3 = _split3(gdec)
    lcum = lcum_ref[...]
    gc_scr[...] = _dot(lcum, g1) + _dot(lcum, g2) + _dot(lcum, g3)

    ri = lax.broadcasted_iota(jnp.int32, (CHUNK, CHUNK), 0)
    ci = lax.broadcasted_iota(jnp.int32, (CHUNK, CHUNK), 1)
    tril = ci <= ri
    tril_strict = ci < ri
    eye = (ci == ri).astype(F32)
    neg_inf = jnp.float32(-jnp.inf)

    def chunk_body(c, carry):
        r0 = pl.multiple_of(c * CHUNK, CHUNK)
        rows = pl.ds(r0, CHUNK)
        gc = gc_scr[rows, :]
        gc_t = jnp.transpose(gc)
        beta = beta_scr[rows, :]
        for h in range(DN_HEADS):
            q = qkv_scr[rows, h * DN_DIM:(h + 1) * DN_DIM]
            k = qkv_scr[rows, DN_WIDTH + h * DN_DIM:DN_WIDTH + (h + 1) * DN_DIM]
            v = qkv_scr[rows, 2 * DN_WIDTH + h * DN_DIM:2 * DN_WIDTH + (h + 1) * DN_DIM]
            b_col = beta[:, h:h + 1]
            g_col = gc[:, DN_HEADS + h:DN_HEADS + h + 1]
            g_row = gc_t[DN_HEADS + h:DN_HEADS + h + 1, :]
            g_last = g_col[CHUNK - 1:CHUNK, :]
            decay = jnp.exp(jnp.where(tril, g_col - g_row, neg_inf))
            e_g = jnp.exp(g_col)
            kb = k * b_col
            k16 = k.astype(BF16)
            a_mat = jnp.where(tril_strict, _dot_nt(kb.astype(BF16), k16) * decay, 0.0)
            pw = -a_mat
            t_mat = eye + pw
            for _ in range(5):
                pw16 = pw.astype(BF16)
                pw = _dot(pw16, pw16)
                t_mat = t_mat + _dot(t_mat.astype(BF16), pw.astype(BF16))
            t16 = t_mat.astype(BF16)
            u = _dot(t16, (v * b_col).astype(BF16))
            w = _dot(t16, (kb * e_g).astype(BF16))
            attn = jnp.where(tril, _dot_nt(q.astype(BF16), k16) * decay, 0.0)
            state = state_scr[h]
            s16 = state.astype(BF16)
            v_new = u - _dot(w.astype(BF16), s16)
            vn16 = v_new.astype(BF16)
            o = _dot((q * e_g).astype(BF16), s16) + _dot(attn.astype(BF16), vn16)
            k_tail = k * jnp.exp(g_last - g_col)
            state_scr[h] = state * jnp.exp(g_last) + _dot_tn(k_tail.astype(BF16), vn16)
            o = o * lax.rsqrt(jnp.mean(o * o, axis=-1, keepdims=True) + EPS) * ognorm_ref[...]
            z = z_ref[rows, h * DN_DIM:(h + 1) * DN_DIM].astype(F32)
            o_ref[rows, h * DN_DIM:(h + 1) * DN_DIM] = (o * _silu(z)).astype(BF16)
        return carry

    lax.fori_loop(0, n_chunks, chunk_body, 0)


def _deltanet(proj3, ab3, conv_w, a_log_l, dtb_l, og_norm, tile):
    b, s, w = proj3.shape
    lcum = jnp.tril(jnp.ones((tile, tile), F32))
    same_chunk = (jnp.arange(tile)[:, None] // CHUNK) == (jnp.arange(tile)[None, :] // CHUNK)
    lcum = jnp.where(same_chunk, lcum, 0.0).astype(BF16)
    return pl.pallas_call(
        functools.partial(_deltanet_kernel, tile=tile),
        grid=(b, s // tile),
        in_specs=[
            pl.BlockSpec((None, tile, 3 * DN_WIDTH), lambda bi, t: (bi, t, COL_B // (3 * DN_WIDTH))),
            pl.BlockSpec((None, tile, DN_WIDTH), lambda bi, t: (bi, t, COL_Z // DN_WIDTH)),
            pl.BlockSpec((None, tile, LANES), lambda bi, t: (bi, t, 0)),
            pl.BlockSpec((CONV_WIDTH, 3 * DN_WIDTH), lambda bi, t: (0, 0)),
            pl.BlockSpec((1, LANES), lambda bi, t: (0, 0)),
            pl.BlockSpec((1, LANES), lambda bi, t: (0, 0)),
            pl.BlockSpec((1, DN_DIM), lambda bi, t: (0, 0)),
            pl.BlockSpec((tile, tile), lambda bi, t: (0, 0)),
        ],
        out_specs=pl.BlockSpec((None, tile, DN_WIDTH), lambda bi, t: (bi, t, 0)),
        out_shape=jax.ShapeDtypeStruct((b, s, DN_WIDTH), BF16),
        scratch_shapes=[
            pltpu.VMEM((tile, 3 * DN_WIDTH), F32),
            pltpu.VMEM((tile, LANES), F32),
            pltpu.VMEM((tile, LANES), F32),
            pltpu.VMEM((8, 3 * DN_WIDTH), F32),
            pltpu.VMEM((DN_HEADS, DN_DIM, DN_DIM), F32),
        ],
        compiler_params=_params(("parallel", "arbitrary")),
        name="deltanet",
    )(proj3, proj3, ab3, conv_w, a_log_l, dtb_l, og_norm, lcum)


def _merge_kernel(o0_ref, o1_ref, o2_ref, l0_ref, l1_ref, l2_ref, ob_ref, ga_ref, gb_ref, x_ref,
                  wa_ref, wb_ref, wo_ref, hexp_ref, gain_ref, wr_hi_ref, wr_lo_ref, rbias_ref,
                  x1_ref, h2_ref, logit_ref):
    l0, l1, l2 = l0_ref[...], l1_ref[...], l2_ref[...]
    m = jnp.maximum(jnp.maximum(l0, l1), l2)
    e0, e1, e2 = jnp.exp(l0 - m), jnp.exp(l1 - m), jnp.exp(l2 - m)
    tot = e0 + e1 + e2
    hexp = hexp_ref[...]
    o_a = jnp.zeros(o0_ref.shape, F32)
    for e, o_ref in ((e0, o0_ref), (e1, o1_ref), (e2, o2_ref)):
        w_hi, w_lo = _split2(e / tot)
        o_a = o_a + (_dot(w_hi, hexp) + _dot(w_lo, hexp)) * o_ref[...].astype(F32)
    ya = _dot(o_a.astype(BF16), wa_ref[...])
    yb = _dot(ob_ref[...], wb_ref[...])
    merged = _sigmoid(ga_ref[...].astype(F32)) * ya + _sigmoid(gb_ref[...].astype(F32)) * yb
    x1 = x_ref[...] + _dot(merged.astype(BF16), wo_ref[...])
    x1_ref[...] = x1
    h2 = x1 * lax.rsqrt(jnp.mean(x1 * x1, axis=-1, keepdims=True) + EPS) * gain_ref[...]
    hi, lo = _split2(h2)
    h2_ref[...] = hi
    logit_ref[...] = (_dot(hi, wr_hi_ref[...]) + _dot(lo, wr_hi_ref[...]) + _dot(hi, wr_lo_ref[...])
                      + rbias_ref[...])


def _merge(o_list, lse_list, o_b, proj, x2, wa, wb, wo, gain, wr_hi, wr_lo, rbias, tm):
    t, d = x2.shape
    head_of_col = jnp.arange(GROUP_WIDTH) // HEAD_DIM_A
    hexp = (jnp.arange(LANES)[:, None] == head_of_col[None, :]).astype(BF16)

    def row(wd, cblk=0):
        return pl.BlockSpec((tm, wd), lambda i: (i, cblk))

    def full(a):
        return pl.BlockSpec(a.shape, lambda i: (0, 0))

    return pl.pallas_call(
        _merge_kernel,
        grid=(t // tm,),
        in_specs=[row(GROUP_WIDTH)] * 3 + [row(LANES)] * 3 + [
            row(DN_WIDTH), row(d, COL_GA // d), row(d, COL_GB // d), row(d),
            full(wa), full(wb), full(wo), full(hexp), full(gain), full(wr_hi), full(wr_lo), full(rbias)],
        out_specs=[row(d), row(d), row(LANES)],
        out_shape=[
            jax.ShapeDtypeStruct((t, d), F32),
            jax.ShapeDtypeStruct((t, d), BF16),
            jax.ShapeDtypeStruct((t, LANES), F32),
        ],
        compiler_params=_params(("parallel",)),
        name="merge",
    )(*o_list, *lse_list, o_b, proj, proj, x2, wa, wb, wo, hexp, gain, wr_hi, wr_lo, rbias)


def _routing_weights(logits):
    lane_i = lax.broadcasted_iota(jnp.int32, logits.shape, 1)
    lane = lane_i.astype(F32)
    neg_inf = jnp.float32(-jnp.inf)
    big = jnp.float32(1 << 20)
    is_group = lane_i < N_EXPERT_GROUPS
    gl = jnp.where(is_group, logits, neg_inf)
    gmax = jnp.max(gl, axis=-1, keepdims=True)
    gsum = jnp.sum(jnp.exp(gl - gmax), axis=-1, keepdims=True)
    p_g = 1.0 / gsum
    g_idx = jnp.min(jnp.where(gl == gmax, lane, big), axis=-1, keepdims=True)
    n_exp = N_EXPERT_GROUPS * EXPERTS_PER_GROUP
    is_exp = jnp.logical_and(lane_i >= N_EXPERT_GROUPS, lane_i < N_EXPERT_GROUPS + n_exp)
    exp_group = jnp.right_shift(lane_i - N_EXPERT_GROUPS, 2).astype(F32)
    assert EXPERTS_PER_GROUP == 4
    sel = jnp.logical_and(is_exp, exp_group == g_idx)
    el = jnp.where(sel, logits, neg_inf)
    emax = jnp.max(el, axis=-1, keepdims=True)
    ee = jnp.exp(el - emax)
    esum = jnp.sum(ee, axis=-1, keepdims=True)
    idx1 = jnp.min(jnp.where(el == emax, lane, big), axis=-1, keepdims=True)
    el2 = jnp.where(lane == idx1, neg_inf, el)
    e2max = jnp.max(el2, axis=-1, keepdims=True)
    idx2 = jnp.min(jnp.where(el2 == e2max, lane, big), axis=-1, keepdims=True)
    p1 = 1.0 / esum
    p2 = jnp.exp(e2max - emax) / esum
    top_sum = p1 + p2
    w = jnp.where(lane == idx1, p1 / top_sum, jnp.where(lane == idx2, p2 / top_sum, 0.0))
    return w * p_g


def _moe_kernel(h2_ref, logit_ref, x1_ref, wg_ref, wu_ref, wd_ref, out_ref, cw_scr, acc_scr):
    g = pl.program_id(1)

    @pl.when(g == 0)
    def _():
        cw_scr[...] = _routing_weights(logit_ref[...])
        acc_scr[...] = jnp.zeros_like(acc_scr)

    h2 = h2_ref[...]
    hg = _dot(h2, wg_ref[...])
    hu = _dot(h2, wu_ref[...])
    cw = cw_scr[...]
    lane = lax.broadcasted_iota(jnp.int32, cw.shape, 1)
    parts = []
    for j in range(EXPERTS_PER_GROUP):
        cols = slice(j * D_EXPERT, (j + 1) * D_EXPERT)
        want = N_EXPERT_GROUPS + g * EXPERTS_PER_GROUP + j
        c_col = jnp.sum(jnp.where(lane == want, cw, 0.0), axis=-1, keepdims=True)
        parts.append((_silu(hg[:, cols]) * hu[:, cols] * c_col).astype(BF16))
    act = jnp.concatenate(parts, axis=-1)
    acc_scr[...] += _dot(act, wd_ref[...])

    @pl.when(g == N_EXPERT_GROUPS - 1)
    def _():
        out_ref[...] = x1_ref[...] + acc_scr[...]


def _moe(h2, logits, x1, wg, wu, wd, tm):
    t, d = x1.shape
    gw = EXPERTS_PER_GROUP * D_EXPERT
    return pl.pallas_call(
        _moe_kernel,
        grid=(t // tm, N_EXPERT_GROUPS),
        in_specs=[
            pl.BlockSpec((tm, d), lambda i, g: (i, 0)),
            pl.BlockSpec((tm, LANES), lambda i, g: (i, 0)),
            pl.BlockSpec((tm, d), lambda i, g: (i, 0)),
            pl.BlockSpec((None, d, gw), lambda i, g: (g, 0, 0)),
            pl.BlockSpec((None, d, gw), lambda i, g: (g, 0, 0)),
            pl.BlockSpec((None, gw, d), lambda i, g: (g, 0, 0)),
        ],
        out_specs=pl.BlockSpec((tm, d), lambda i, g: (i, 0)),
        out_shape=jax.ShapeDtypeStruct((t, d), F32),
        scratch_shapes=[pltpu.VMEM((tm, LANES), F32), pltpu.VMEM((tm, d), F32)],
        compiler_params=_params(("parallel", "arbitrary")),
        name="moe",
    )(h2, logits, x1, wg, wu, wd)


def _ple_kernel(x_ref, p_ref, gain_ref, wpg_ref, wple_ref, out_ref):
    x = x_ref[...]
    h3 = x * lax.rsqrt(jnp.mean(x * x, axis=-1, keepdims=True) + EPS) * gain_ref[...]
    gate = _sigmoid(_dot(h3.astype(BF16), wpg_ref[...]))
    ple = _dot(p_ref[...].astype(BF16), wple_ref[...])
    out_ref[...] = x + gate * ple


def _ple(x2, p2, gain, wpg, wple, tm):
    t, d = x2.shape
    pd = p2.shape[1]
    return pl.pallas_call(
        _ple_kernel,
        grid=(t // tm,),
        in_specs=[
            pl.BlockSpec((tm, d), lambda i: (i, 0)),
            pl.BlockSpec((tm, pd), lambda i: (i, 0)),
            pl.BlockSpec((1, d), lambda i: (0, 0)),
            pl.BlockSpec((d, d), lambda i: (0, 0)),
            pl.BlockSpec((pd, d), lambda i: (0, 0)),
        ],
        out_specs=pl.BlockSpec((tm, d), lambda i: (i, 0)),
        out_shape=jax.ShapeDtypeStruct((t, d), F32),
        compiler_params=_params(("parallel",)),
        name="ple",
    )(x2, p2, gain, wpg, wple)


def _lane_pad(v, offset):
    out = jnp.zeros((1, LANES), F32)
    return out.at[0, offset:offset + v.shape[0]].set(v.astype(F32))


def _layer(x2, p2, b, s, norm_mix, w_in, q_norm, k_norm, conv_w, a_log, dt_bias, dn_out_norm,
           w_branch_a, w_branch_b, w_out, norm_ffn, w_router_group, b_router_group,
           w_router_expert, b_router_expert, w_expert_gate, w_expert_up, w_expert_down,
           norm_ple, w_ple, w_ple_gate):
    t, d = x2.shape
    qa = N_GROUPS * GROUP_WIDTH
    o_b0 = 3 * qa
    o_z = o_b0 + 3 * DN_WIDTH
    o_beta = o_z + DN_WIDTH
    o_alpha = o_beta + DN_HEADS
    o_ga = o_alpha + DN_HEADS
    o_gb = o_ga + d
    w_in = w_in.astype(F32)
    w_main = jnp.concatenate([
        w_in[:, 0:2 * qa],
        w_in[:, o_b0:o_z],
        w_in[:, o_z:o_beta],
        w_in[:, o_ga:o_gb],
        w_in[:, o_gb:o_gb + d],
        w_in[:, 2 * qa:3 * qa],
    ], axis=1).astype(BF16)
    w_ab = jnp.zeros((d, LANES), F32).at[:, 0:2 * DN_HEADS].set(w_in[:, o_beta:o_ga])
    wab_hi, wab_lo = _split2(w_ab)
    qg = jnp.tile(q_norm.astype(F32), N_GROUPS * HEADS_A) * (HEAD_DIM_A ** -0.5)
    kg = jnp.tile(k_norm.astype(F32), N_GROUPS * HEADS_A)
    qkgain = jnp.concatenate([qg, kg])[None, :]

    proj, ab = _in_projection(x2, norm_mix.astype(F32)[None, :], w_main, wab_hi, wab_lo, qkgain,
                              tm=min(1024, t))
    proj3 = proj.reshape(b, s, PROJ_WIDTH)

    o_list, lse_list = [], []
    for g, (_, dil) in enumerate(ATTN_GROUPS):
        o_g, lse_g = _attention_group(proj3, g, dil, tq=512)
        o_list.append(o_g)
        lse_list.append(lse_g)

    o_b = _deltanet(proj3, ab.reshape(b, s, LANES), conv_w.astype(F32),
                    _lane_pad(a_log, DN_HEADS), _lane_pad(dt_bias, DN_HEADS),
                    dn_out_norm.astype(F32)[None, :], tile=min(512, s)).reshape(t, DN_WIDTH)

    w_r = jnp.zeros((d, LANES), F32)
    w_r = w_r.at[:, 0:N_EXPERT_GROUPS].set(w_router_group.astype(F32))
    n_exp = N_EXPERT_GROUPS * EXPERTS_PER_GROUP
    w_r = w_r.at[:, N_EXPERT_GROUPS:N_EXPERT_GROUPS + n_exp].set(w_router_expert.astype(F32))
    wr_hi, wr_lo = _split2(w_r)
    rbias = _lane_pad(jnp.concatenate([b_router_group.astype(F32), b_router_expert.astype(F32)]), 0)

    x1, h2, logits = _merge(o_list, lse_list, o_b, proj, x2, w_branch_a.astype(BF16),
                            w_branch_b.astype(BF16), w_out.astype(BF16), norm_ffn.astype(F32)[None, :],
                            wr_hi, wr_lo, rbias, tm=min(512, t))

    gw = EXPERTS_PER_GROUP * D_EXPERT
    wg = jnp.transpose(w_expert_gate, (0, 2, 1, 3)).reshape(N_EXPERT_GROUPS, d, gw).astype(BF16)
    wu = jnp.transpose(w_expert_up, (0, 2, 1, 3)).reshape(N_EXPERT_GROUPS, d, gw).astype(BF16)
    wd = w_expert_down.reshape(N_EXPERT_GROUPS, gw, d).astype(BF16)
    x2n = _moe(h2, logits, x1, wg, wu, wd, tm=min(1024, t))

    return _ple(x2n, p2, norm_ple.astype(F32)[None, :], w_ple_gate.astype(BF16), w_ple.astype(BF16),
                tm=min(1024, t))


def kernel(x, p, norm_mix, w_in, q_norm, k_norm, conv_w, a_log, dt_bias, dn_out_norm, w_branch_a, w_branch_b, w_out, norm_ffn, w_router_group, b_router_group, w_router_expert, b_router_expert, w_expert_gate, w_expert_up, w_expert_down, norm_ple, w_ple, w_ple_gate):
    b, s, d = x.shape
    depth = w_in.shape[0]
    x2 = x.astype(F32).reshape(b * s, d)
    for i in range(depth):
        x2 = _layer(x2, p[i].reshape(b * s, -1), b, s, norm_mix[i], w_in[i], q_norm[i], k_norm[i],
                    conv_w[i], a_log[i], dt_bias[i], dn_out_norm[i], w_branch_a[i], w_branch_b[i],
                    w_out[i], norm_ffn[i], w_router_group[i], b_router_group[i], w_router_expert[i],
                    b_router_expert[i], w_expert_gate[i], w_expert_up[i], w_expert_down[i],
                    norm_ple[i], w_ple[i], w_ple_gate[i])
    return x2.reshape(b, s, d)
```

```python
import functools

import jax
import jax.numpy as jnp
from jax import lax
from jax.experimental import pallas as pl
from jax.experimental.pallas import tpu as pltpu

F32 = jnp.float32
BF16 = jnp.bfloat16
EPS = 1e-6

ATTN_GROUPS = ((128, 1), (512, 4), (2048, 16))
N_GROUPS = 3
HEADS_A = 8
HEAD_DIM_A = 64
GROUP_WIDTH = HEADS_A * HEAD_DIM_A
ATTN_BLOCK = 128
DN_HEADS = 8
DN_DIM = 128
DN_WIDTH = DN_HEADS * DN_DIM
CONV_WIDTH = 4
CHUNK = 64
N_EXPERT_GROUPS = 4
EXPERTS_PER_GROUP = 4
D_EXPERT = 256
LANES = 128

PROJ_TILE = 1536
COL_Q, COL_K, COL_B, COL_Z, COL_GA, COL_GB, COL_V = 0, 1536, 3072, 6144, 7168, 8192, 9216
PROJ_WIDTH = 10752

VMEM_LIMIT = 56 * 1024 * 1024


def _dot(a, b):
    return jnp.dot(a, b, preferred_element_type=F32)


def _dot_nt(a, b):
    return lax.dot_general(a, b, (((1,), (1,)), ((), ())), preferred_element_type=F32)


def _dot_tn(a, b):
    return lax.dot_general(a, b, (((0,), (0,)), ((), ())), preferred_element_type=F32)


def _split2(v):
    hi = v.astype(BF16)
    lo = (v - hi.astype(F32)).astype(BF16)
    return hi, lo


def _split3(v):
    a = v.astype(BF16)
    r = v - a.astype(F32)
    b = r.astype(BF16)
    c = (r - b.astype(F32)).astype(BF16)
    return a, b, c


def _sigmoid(v):
    return 1.0 / (1.0 + jnp.exp(-v))


def _silu(v):
    return v / (1.0 + jnp.exp(-v))


def _params(sem):
    return pltpu.CompilerParams(dimension_semantics=sem, vmem_limit_bytes=VMEM_LIMIT)


def _inproj_kernel(x_ref, gain_ref, w_ref, wab_hi_ref, wab_lo_ref, qkgain_ref, red_ref, exp_ref,
                   proj_ref, ab_ref, p1_ref, p2_ref, h_scr, perm_scr, *, tm, n_j):
    j = pl.program_id(1)

    def emit_attention_tile(groups):
        for g, val in enumerate(groups):
            proj_ref[:, g * GROUP_WIDTH:(g + 1) * GROUP_WIDTH] = val.astype(BF16)
        for g, out_ref in ((1, p1_ref), (2, p2_ref)):
            dil = ATTN_GROUPS[g][1]
            for c in range(GROUP_WIDTH // LANES):
                perm_scr[c] = groups[g][:, c * LANES:(c + 1) * LANES]
            for r in range(dil):
                for c in range(GROUP_WIDTH // LANES):
                    out_ref[r, :, c * LANES:(c + 1) * LANES] = (
                        perm_scr[c, pl.ds(r, tm // dil, stride=dil), :].astype(BF16))

    @pl.when(j == 0)
    def _():
        x = x_ref[...]
        ms = jnp.mean(x * x, axis=-1, keepdims=True)
        h = x * lax.rsqrt(ms + EPS) * gain_ref[...]
        hi, lo = _split2(h)
        h_scr[...] = hi
        ab_ref[...] = _dot(hi, wab_hi_ref[...]) + _dot(lo, wab_hi_ref[...]) + _dot(hi, wab_lo_ref[...])

    acc = _dot(h_scr[...], w_ref[...])

    @pl.when(j < 2)
    def _():
        groups = []
        for g in range(N_GROUPS):
            cols = slice(g * GROUP_WIDTH, (g + 1) * GROUP_WIDTH)
            a = acc[:, cols]
            ss = _dot((a * a).astype(BF16), red_ref[...])
            r = lax.rsqrt(ss * (1.0 / HEAD_DIM_A) + EPS)
            r_hi, r_lo = _split2(r)
            rexp = _dot(r_hi, exp_ref[...]) + _dot(r_lo, exp_ref[...])
            groups.append(a * rexp * qkgain_ref[:, cols])
        emit_attention_tile(groups)

    @pl.when(j == n_j - 1)
    def _():
        emit_attention_tile([acc[:, g * GROUP_WIDTH:(g + 1) * GROUP_WIDTH] for g in range(N_GROUPS)])

    @pl.when(jnp.logical_and(j >= 2, j < n_j - 1))
    def _():
        proj_ref[...] = acc.astype(BF16)


def _in_projection(x2, norm_gain, w_main, wab_hi, wab_lo, qkgain, b, s, tm):
    t, d = x2.shape
    n_j = PROJ_WIDTH // PROJ_TILE
    tps = s // tm
    col = jnp.arange(GROUP_WIDTH) // HEAD_DIM_A
    red = (col[:, None] == jnp.arange(LANES)[None, :]).astype(BF16)
    expm = red.T
    d1, d2 = ATTN_GROUPS[1][1], ATTN_GROUPS[2][1]

    def slab(j):
        return jnp.where(j == 0, 0, jnp.where(j < n_j - 1, 1, 2))

    def perm_spec(dil):
        return pl.BlockSpec((None, dil, tm // dil, GROUP_WIDTH),
                            lambda i, j: (i // tps, 0, i % tps, slab(j)))

    return pl.pallas_call(
        functools.partial(_inproj_kernel, tm=tm, n_j=n_j),
        grid=(t // tm, n_j),
        in_specs=[
            pl.BlockSpec((tm, d), lambda i, j: (i, 0)),
            pl.BlockSpec((1, d), lambda i, j: (0, 0)),
            pl.BlockSpec((d, PROJ_TILE), lambda i, j: (0, j)),
            pl.BlockSpec((d, LANES), lambda i, j: (0, 0)),
            pl.BlockSpec((d, LANES), lambda i, j: (0, 0)),
            pl.BlockSpec((1, PROJ_TILE), lambda i, j: (0, jnp.minimum(j, 1))),
            pl.BlockSpec((GROUP_WIDTH, LANES), lambda i, j: (0, 0)),
            pl.BlockSpec((LANES, GROUP_WIDTH), lambda i, j: (0, 0)),
        ],
        out_specs=[
            pl.BlockSpec((tm, PROJ_TILE), lambda i, j: (i, j)),
            pl.BlockSpec((tm, LANES), lambda i, j: (i, 0)),
            perm_spec(d1),
            perm_spec(d2),
        ],
        out_shape=[
            jax.ShapeDtypeStruct((t, PROJ_WIDTH), BF16),
            jax.ShapeDtypeStruct((t, LANES), F32),
            jax.ShapeDtypeStruct((b, d1, s // d1, 3 * GROUP_WIDTH), BF16),
            jax.ShapeDtypeStruct((b, d2, s // d2, 3 * GROUP_WIDTH), BF16),
        ],
        scratch_shapes=[pltpu.VMEM((tm, d), BF16), pltpu.VMEM((GROUP_WIDTH // LANES, tm, LANES), F32)],
        compiler_params=_params(("parallel", "arbitrary")),
        name="in_proj",
    )(x2, norm_gain, w_main, wab_hi, wab_lo, qkgain, red, expm)


def _attn_kernel(q_ref, kc_ref, kp_ref, vc_ref, vp_ref, o_ref, lse_ref, *, n_blk):
    n = pl.program_id(2)
    blk = ATTN_BLOCK
    qi = lax.broadcasted_iota(jnp.int32, (blk, blk), 0)
    kj = lax.broadcasted_iota(jnp.int32, (blk, blk), 1)
    cur_ok = kj <= qi
    prev_band = kj >= qi
    lane = lax.broadcasted_iota(jnp.int32, (blk, LANES), 1)
    low_half = lane < HEAD_DIM_A
    neg_inf = jnp.float32(-jnp.inf)
    ones = jnp.ones((blk, LANES), BF16)
    zero = jnp.zeros((blk, LANES), BF16)

    for b in range(n_blk):
        rows = slice(b * blk, (b + 1) * blk)
        if b == 0:
            prev_ok = jnp.logical_and(prev_band, n > 0)
        else:
            prev_ok = prev_band
        lse_tile = jnp.zeros((blk, LANES), F32)
        for pair in range(HEADS_A // 2):
            cols = slice(pair * LANES, (pair + 1) * LANES)
            q2 = q_ref[rows, cols]
            kc2 = kc_ref[rows, cols]
            vc2 = vc_ref[rows, cols]
            if b == 0:
                kp2 = kp_ref[:, cols]
                vp2 = vp_ref[:, cols]
            else:
                prow = slice((b - 1) * blk, b * blk)
                kp2 = kc_ref[prow, cols]
                vp2 = vc_ref[prow, cols]
            vc_aug = jnp.concatenate([vc2, ones], axis=1)
            vp_aug = jnp.concatenate([vp2, ones], axis=1)
            q_heads = (jnp.where(low_half, q2, zero), jnp.where(low_half, zero, q2))
            s_c = [jnp.where(cur_ok, _dot_nt(qh, kc2), neg_inf) for qh in q_heads]
            s_p = [jnp.where(prev_ok, _dot_nt(qh, kp2), neg_inf) for qh in q_heads]
            m = [jnp.maximum(jnp.max(sc, axis=-1, keepdims=True), jnp.max(sp, axis=-1, keepdims=True))
                 for sc, sp in zip(s_c, s_p)]
            e_c = [jnp.exp(sc - mh).astype(BF16) for sc, mh in zip(s_c, m)]
            e_p = [jnp.exp(sp - mh).astype(BF16) for sp, mh in zip(s_p, m)]
            nd = [_dot(ec, vc_aug) + _dot(ep, vp_aug) for ec, ep in zip(e_c, e_p)]
            outs = [x[:, :LANES] / x[:, LANES:] for x in nd]
            for half in range(2):
                head = 2 * pair + half
                lse_tile = jnp.where(lane == head, m[half] + jnp.log(nd[half][:, LANES:LANES + 1]), lse_tile)
            o_ref[rows, cols] = jnp.where(low_half, outs[0], outs[1]).astype(BF16)
        lse_ref[rows, :] = lse_tile


def _attention_group(arr4, cq, ck, cv, tq):
    b, dil, n_sub, _ = arr4.shape
    tq = min(tq, n_sub)
    n_blk = tq // ATTN_BLOCK

    def cur(c0):
        return pl.BlockSpec((None, None, tq, GROUP_WIDTH), lambda bi, r, n: (bi, r, n, c0))

    def prev(c0):
        return pl.BlockSpec((None, None, ATTN_BLOCK, GROUP_WIDTH),
                            lambda bi, r, n: (bi, r, jnp.maximum(n * n_blk - 1, 0), c0))

    return pl.pallas_call(
        functools.partial(_attn_kernel, n_blk=n_blk),
        grid=(b, dil, n_sub // tq),
        in_specs=[cur(cq), cur(ck), prev(ck), cur(cv), prev(cv)],
        out_specs=[
            pl.BlockSpec((None, None, tq, GROUP_WIDTH), lambda bi, r, n: (bi, r, n, 0)),
            pl.BlockSpec((None, None, tq, LANES), lambda bi, r, n: (bi, r, n, 0)),
        ],
        out_shape=[
            jax.ShapeDtypeStruct((b, dil, n_sub, GROUP_WIDTH), BF16),
            jax.ShapeDtypeStruct((b, dil, n_sub, LANES), F32),
        ],
        compiler_params=_params(("parallel", "parallel", "parallel")),
        name=f"attn_d{dil}",
    )(arr4, arr4, arr4, arr4, arr4)


def _deltanet_kernel(xb_ref, z_ref, ab_ref, convw_ref, alog_ref, dtb_ref, ognorm_ref, lcum_ref,
                     o_ref, qkv_scr, gc_scr, beta_scr, carry_scr, state_scr, ext_scr, *, tile):
    t_idx = pl.program_id(1)
    n_chunks = tile // CHUNK
    halo = 8
    heads = range(DN_HEADS)

    @pl.when(t_idx == 0)
    def _():
        carry_scr[...] = jnp.zeros_like(carry_scr)
        state_scr[...] = jnp.zeros_like(state_scr)

    for c in range(3 * DN_HEADS):
        cols = slice(c * DN_DIM, (c + 1) * DN_DIM)
        cur = xb_ref[:, cols].astype(F32)
        ext_scr[c, 0:halo, :] = carry_scr[:, cols]
        ext_scr[c, halo:halo + tile, :] = cur
        y = cur * convw_ref[CONV_WIDTH - 1:CONV_WIDTH, cols]
        for j in range(CONV_WIDTH - 1):
            off = halo - (CONV_WIDTH - 1) + j
            y = y + ext_scr[c, pl.ds(off, tile), :] * convw_ref[j:j + 1, cols]
        carry_scr[:, cols] = cur[tile - halo:tile]
        y = _silu(y)
        if c < 2 * DN_HEADS:
            y = y * lax.rsqrt(jnp.sum(y * y, axis=-1, keepdims=True) + EPS)
            if c < DN_HEADS:
                y = y * (DN_DIM ** -0.5)
        qkv_scr[:, cols] = y

    ab = ab_ref[...]
    beta_scr[...] = _sigmoid(ab)
    sp_in = ab + dtb_ref[...]
    softplus = jnp.maximum(sp_in, 0.0) + jnp.log(1.0 + jnp.exp(-jnp.abs(sp_in)))
    gdec = -jnp.exp(alog_ref[...]) * softplus
    g1, g2, g3 = _split3(gdec)
    lcum = lcum_ref[...]
    gc_scr[...] = _dot(lcum, g1) + _dot(lcum, g2) + _dot(lcum, g3)

    ri = lax.broadcasted_iota(jnp.int32, (CHUNK, CHUNK), 0)
    ci = lax.broadcasted_iota(jnp.int32, (CHUNK, CHUNK), 1)
    tril = ci <= ri
    tril_strict = ci < ri
    eye = (ci == ri).astype(F32)
    neg_inf = jnp.float32(-jnp.inf)

    def chunk_body(c, carry):
        r0 = pl.multiple_of(c * CHUNK, CHUNK)
        rows = pl.ds(r0, CHUNK)
        gc = gc_scr[rows, :]
        gc_t = jnp.transpose(gc)
        beta = beta_scr[rows, :]
        q = [qkv_scr[rows, h * DN_DIM:(h + 1) * DN_DIM] for h in heads]
        k = [qkv_scr[rows, DN_WIDTH + h * DN_DIM:DN_WIDTH + (h + 1) * DN_DIM] for h in heads]
        v = [qkv_scr[rows, 2 * DN_WIDTH + h * DN_DIM:2 * DN_WIDTH + (h + 1) * DN_DIM] for h in heads]
        z = [z_ref[rows, h * DN_DIM:(h + 1) * DN_DIM] for h in heads]
        state = [state_scr[h] for h in heads]

        b_col = [beta[:, h:h + 1] for h in heads]
        g_col = [gc[:, DN_HEADS + h:DN_HEADS + h + 1] for h in heads]
        g_row = [gc_t[DN_HEADS + h:DN_HEADS + h + 1, :] for h in heads]
        g_last = [gcol[CHUNK - 1:CHUNK, :] for gcol in g_col]
        decay = [jnp.exp(jnp.where(tril, gcol - grow, neg_inf)) for gcol, grow in zip(g_col, g_row)]
        e_g = [jnp.exp(gcol) for gcol in g_col]
        kb = [kh * bh for kh, bh in zip(k, b_col)]
        k16 = [kh.astype(BF16) for kh in k]
        a_mat = [jnp.where(tril_strict, _dot_nt(kbh.astype(BF16), kh16) * dh, 0.0)
                 for kbh, kh16, dh in zip(kb, k16, decay)]
        pw = [-a for a in a_mat]
        t_mat = [eye + p for p in pw]
        for _ in range(5):
            pw16 = [p.astype(BF16) for p in pw]
            pw = [_dot(p, p) for p in pw16]
            t_mat = [th + _dot(th.astype(BF16), p.astype(BF16)) for th, p in zip(t_mat, pw)]
        t16 = [th.astype(BF16) for th in t_mat]
        u = [_dot(th, (vh * bh).astype(BF16)) for th, vh, bh in zip(t16, v, b_col)]
        w = [_dot(th, (kbh * eh).astype(BF16)) for th, kbh, eh in zip(t16, kb, e_g)]
        attn = [jnp.where(tril, _dot_nt(qh.astype(BF16), kh16) * dh, 0.0).astype(BF16)
                for qh, kh16, dh in zip(q, k16, decay)]
        q_dec = [(qh * eh).astype(BF16) for qh, eh in zip(q, e_g)]
        k_tail = [(kh * jnp.exp(gl - gcol)).astype(BF16) for kh, gl, gcol in zip(k, g_last, g_col)]

        s16 = [sh.astype(BF16) for sh in state]
        v_new = [(uh - _dot(wh.astype(BF16), sh)).astype(BF16) for uh, wh, sh in zip(u, w, s16)]
        o = [_dot(qd, sh) + _dot(ah, vn) for qd, sh, ah, vn in zip(q_dec, s16, attn, v_new)]
        new_state = [sh * jnp.exp(gl) + _dot_tn(kt, vn)
                     for sh, gl, kt, vn in zip(state, g_last, k_tail, v_new)]
        o = [oh * lax.rsqrt(jnp.mean(oh * oh, axis=-1, keepdims=True) + EPS) * ognorm_ref[...] for oh in o]
        o = [(oh * _silu(zh.astype(F32))).astype(BF16) for oh, zh in zip(o, z)]
        for h in heads:
            state_scr[h] = new_state[h]
        for h in heads:
            o_ref[rows, h * DN_DIM:(h + 1) * DN_DIM] = o[h]
        return carry

    lax.fori_loop(0, n_chunks, chunk_body, 0)


def _deltanet(proj3, ab3, conv_w, a_log_l, dtb_l, og_norm, tile):
    b, s, w = proj3.shape
    lcum = jnp.tril(jnp.ones((tile, tile), F32))
    same_chunk = (jnp.arange(tile)[:, None] // CHUNK) == (jnp.arange(tile)[None, :] // CHUNK)
    lcum = jnp.where(same_chunk, lcum, 0.0).astype(BF16)
    return pl.pallas_call(
        functools.partial(_deltanet_kernel, tile=tile),
        grid=(b, s // tile),
        in_specs=[
            pl.BlockSpec((None, tile, 3 * DN_WIDTH), lambda bi, t: (bi, t, COL_B // (3 * DN_WIDTH))),
            pl.BlockSpec((None, tile, DN_WIDTH), lambda bi, t: (bi, t, COL_Z // DN_WIDTH)),
            pl.BlockSpec((None, tile, LANES), lambda bi, t: (bi, t, 0)),
            pl.BlockSpec((CONV_WIDTH, 3 * DN_WIDTH), lambda bi, t: (0, 0)),
            pl.BlockSpec((1, LANES), lambda bi, t: (0, 0)),
            pl.BlockSpec((1, LANES), lambda bi, t: (0, 0)),
            pl.BlockSpec((1, DN_DIM), lambda bi, t: (0, 0)),
            pl.BlockSpec((tile, tile), lambda bi, t: (0, 0)),
        ],
        out_specs=pl.BlockSpec((None, tile, DN_WIDTH), lambda bi, t: (bi, t, 0)),
        out_shape=jax.ShapeDtypeStruct((b, s, DN_WIDTH), BF16),
        scratch_shapes=[
            pltpu.VMEM((tile, 3 * DN_WIDTH), F32),
            pltpu.VMEM((tile, LANES), F32),
            pltpu.VMEM((tile, LANES), F32),
            pltpu.VMEM((8, 3 * DN_WIDTH), F32),
            pltpu.VMEM((DN_HEADS, DN_DIM, DN_DIM), F32),
            pltpu.VMEM((3 * DN_HEADS, 8 + tile, DN_DIM), F32),
        ],
        compiler_params=_params(("parallel", "arbitrary")),
        name="deltanet",
    )(proj3, proj3, ab3, conv_w, a_log_l, dtb_l, og_norm, lcum)


def _merge_kernel(o0_ref, o1_ref, o2_ref, l0_ref, l1_ref, l2_ref, ob_ref, ga_ref, gb_ref, x_ref,
                  wa_ref, wb_ref, wo_ref, hexp_ref, gain_ref, wr_hi_ref, wr_lo_ref, rbias_ref,
                  x1_ref, h2_ref, logit_ref, o1_scr, o2_scr, l1_scr, l2_scr, *, tm):
    for g, src, dst in ((1, o1_ref, o1_scr), (2, o2_ref, o2_scr), (1, l1_ref, l1_scr), (2, l2_ref, l2_scr)):
        dil = ATTN_GROUPS[g][1]
        for r in range(dil):
            for c in range(dst.shape[0]):
                dst[c, pl.ds(r, tm // dil, stride=dil), :] = src[r, :, c * LANES:(c + 1) * LANES].astype(F32)

    def slabs(scr):
        return jnp.concatenate([scr[c] for c in range(scr.shape[0])], axis=-1)

    l0, l1, l2 = l0_ref[...], l1_scr[0], l2_scr[0]
    m = jnp.maximum(jnp.maximum(l0, l1), l2)
    e0, e1, e2 = jnp.exp(l0 - m), jnp.exp(l1 - m), jnp.exp(l2 - m)
    tot = e0 + e1 + e2
    hexp = hexp_ref[...]
    o_a = jnp.zeros(o0_ref.shape, F32)
    for e, o_val in ((e0, o0_ref[...].astype(F32)), (e1, slabs(o1_scr)), (e2, slabs(o2_scr))):
        w_hi, w_lo = _split2(e / tot)
        o_a = o_a + (_dot(w_hi, hexp) + _dot(w_lo, hexp)) * o_val
    ya = _dot(o_a.astype(BF16), wa_ref[...])
    yb = _dot(ob_ref[...], wb_ref[...])
    merged = _sigmoid(ga_ref[...].astype(F32)) * ya + _sigmoid(gb_ref[...].astype(F32)) * yb
    x1 = x_ref[...] + _dot(merged.astype(BF16), wo_ref[...])
    x1_ref[...] = x1
    h2 = x1 * lax.rsqrt(jnp.mean(x1 * x1, axis=-1, keepdims=True) + EPS) * gain_ref[...]
    hi, lo = _split2(h2)
    h2_ref[...] = hi
    logit_ref[...] = (_dot(hi, wr_hi_ref[...]) + _dot(lo, wr_hi_ref[...]) + _dot(hi, wr_lo_ref[...])
                      + rbias_ref[...])


def _merge(o0, l0, o_perm, l_perm, o_b, proj, x2, wa, wb, wo, gain, wr_hi, wr_lo, rbias, s, tm):
    t, d = x2.shape
    tps = s // tm
    head_of_col = jnp.arange(GROUP_WIDTH) // HEAD_DIM_A
    hexp = (jnp.arange(LANES)[:, None] == head_of_col[None, :]).astype(BF16)

    def row(wd, cblk=0):
        return pl.BlockSpec((tm, wd), lambda i: (i, cblk))

    def perm(arr):
        _, dil, _, wd = arr.shape
        return pl.BlockSpec((None, dil, tm // dil, wd), lambda i: (i // tps, 0, i % tps, 0))

    def full(a):
        return pl.BlockSpec(a.shape, lambda i: (0, 0))

    return pl.pallas_call(
        functools.partial(_merge_kernel, tm=tm),
        grid=(t // tm,),
        in_specs=[row(GROUP_WIDTH), perm(o_perm[0]), perm(o_perm[1]),
                  row(LANES), perm(l_perm[0]), perm(l_perm[1]),
                  row(DN_WIDTH), row(d, COL_GA // d), row(d, COL_GB // d), row(d),
                  full(wa), full(wb), full(wo), full(hexp), full(gain), full(wr_hi), full(wr_lo),
                  full(rbias)],
        out_specs=[row(d), row(d), row(LANES)],
        out_shape=[
            jax.ShapeDtypeStruct((t, d), F32),
            jax.ShapeDtypeStruct((t, d), BF16),
            jax.ShapeDtypeStruct((t, LANES), F32),
        ],
        scratch_shapes=[pltpu.VMEM((GROUP_WIDTH // LANES, tm, LANES), F32),
                        pltpu.VMEM((GROUP_WIDTH // LANES, tm, LANES), F32),
                        pltpu.VMEM((1, tm, LANES), F32), pltpu.VMEM((1, tm, LANES), F32)],
        compiler_params=_params(("parallel",)),
        name="merge",
    )(o0, o_perm[0], o_perm[1], l0, l_perm[0], l_perm[1], o_b, proj, proj, x2,
      wa, wb, wo, hexp, gain, wr_hi, wr_lo, rbias)


def _routing_weights(logits):
    lane_i = lax.broadcasted_iota(jnp.int32, logits.shape, 1)
    lane = lane_i.astype(F32)
    neg_inf = jnp.float32(-jnp.inf)
    big = jnp.float32(1 << 20)
    is_group = lane_i < N_EXPERT_GROUPS
    gl = jnp.where(is_group, logits, neg_inf)
    gmax = jnp.max(gl, axis=-1, keepdims=True)
    gsum = jnp.sum(jnp.exp(gl - gmax), axis=-1, keepdims=True)
    p_g = 1.0 / gsum
    g_idx = jnp.min(jnp.where(gl == gmax, lane, big), axis=-1, keepdims=True)
    n_exp = N_EXPERT_GROUPS * EXPERTS_PER_GROUP
    is_exp = jnp.logical_and(lane_i >= N_EXPERT_GROUPS, lane_i < N_EXPERT_GROUPS + n_exp)
    exp_group = jnp.right_shift(lane_i - N_EXPERT_GROUPS, 2).astype(F32)
    assert EXPERTS_PER_GROUP == 4
    sel = jnp.logical_and(is_exp, exp_group == g_idx)
    el = jnp.where(sel, logits, neg_inf)
    emax = jnp.max(el, axis=-1, keepdims=True)
    ee = jnp.exp(el - emax)
    esum = jnp.sum(ee, axis=-1, keepdims=True)
    idx1 = jnp.min(jnp.where(el == emax, lane, big), axis=-1, keepdims=True)
    el2 = jnp.where(lane == idx1, neg_inf, el)
    e2max = jnp.max(el2, axis=-1, keepdims=True)
    idx2 = jnp.min(jnp.where(el2 == e2max, lane, big), axis=-1, keepdims=True)
    p1 = 1.0 / esum
    p2 = jnp.exp(e2max - emax) / esum
    top_sum = p1 + p2
    w = jnp.where(lane == idx1, p1 / top_sum, jnp.where(lane == idx2, p2 / top_sum, 0.0))
    return w * p_g


def _moe_kernel(h2_ref, logit_ref, x1_ref, wg_ref, wu_ref, wd_ref, out_ref, cw_scr, acc_scr):
    g = pl.program_id(1)

    @pl.when(g == 0)
    def _():
        cw_scr[...] = _routing_weights(logit_ref[...])
        acc_scr[...] = jnp.zeros_like(acc_scr)

    h2 = h2_ref[...]
    hg = _dot(h2, wg_ref[...])
    hu = _dot(h2, wu_ref[...])
    cw = cw_scr[...]
    lane = lax.broadcasted_iota(jnp.int32, cw.shape, 1)
    parts = []
    for j in range(EXPERTS_PER_GROUP):
        cols = slice(j * D_EXPERT, (j + 1) * D_EXPERT)
        want = N_EXPERT_GROUPS + g * EXPERTS_PER_GROUP + j
        c_col = jnp.sum(jnp.where(lane == want, cw, 0.0), axis=-1, keepdims=True)
        parts.append((_silu(hg[:, cols]) * hu[:, cols] * c_col).astype(BF16))
    act = jnp.concatenate(parts, axis=-1)
    acc_scr[...] += _dot(act, wd_ref[...])

    @pl.when(g == N_EXPERT_GROUPS - 1)
    def _():
        out_ref[...] = x1_ref[...] + acc_scr[...]


def _moe(h2, logits, x1, wg, wu, wd, tm):
    t, d = x1.shape
    gw = EXPERTS_PER_GROUP * D_EXPERT
    return pl.pallas_call(
        _moe_kernel,
        grid=(t // tm, N_EXPERT_GROUPS),
        in_specs=[
            pl.BlockSpec((tm, d), lambda i, g: (i, 0)),
            pl.BlockSpec((tm, LANES), lambda i, g: (i, 0)),
            pl.BlockSpec((tm, d), lambda i, g: (i, 0)),
            pl.BlockSpec((None, d, gw), lambda i, g: (g, 0, 0)),
            pl.BlockSpec((None, d, gw), lambda i, g: (g, 0, 0)),
            pl.BlockSpec((None, gw, d), lambda i, g: (g, 0, 0)),
        ],
        out_specs=pl.BlockSpec((tm, d), lambda i, g: (i, 0)),
        out_shape=jax.ShapeDtypeStruct((t, d), F32),
        scratch_shapes=[pltpu.VMEM((tm, LANES), F32), pltpu.VMEM((tm, d), F32)],
        compiler_params=_params(("parallel", "arbitrary")),
        name="moe",
    )(h2, logits, x1, wg, wu, wd)


def _ple_kernel(x_ref, p_ref, gain_ref, wpg_ref, wple_ref, out_ref):
    x = x_ref[...]
    h3 = x * lax.rsqrt(jnp.mean(x * x, axis=-1, keepdims=True) + EPS) * gain_ref[...]
    gate = _sigmoid(_dot(h3.astype(BF16), wpg_ref[...]))
    ple = _dot(p_ref[...].astype(BF16), wple_ref[...])
    out_ref[...] = x + gate * ple


def _ple(x2, p2, gain, wpg, wple, tm):
    t, d = x2.shape
    pd = p2.shape[1]
    return pl.pallas_call(
        _ple_kernel,
        grid=(t // tm,),
        in_specs=[
            pl.BlockSpec((tm, d), lambda i: (i, 0)),
            pl.BlockSpec((tm, pd), lambda i: (i, 0)),
            pl.BlockSpec((1, d), lambda i: (0, 0)),
            pl.BlockSpec((d, d), lambda i: (0, 0)),
            pl.BlockSpec((pd, d), lambda i: (0, 0)),
        ],
        out_specs=pl.BlockSpec((tm, d), lambda i: (i, 0)),
        out_shape=jax.ShapeDtypeStruct((t, d), F32),
        compiler_params=_params(("parallel",)),
        name="ple",
    )(x2, p2, gain, wpg, wple)


def _lane_pad(v, offset):
    out = jnp.zeros((1, LANES), F32)
    return out.at[0, offset:offset + v.shape[0]].set(v.astype(F32))


def _layer(x2, p2, b, s, norm_mix, w_in, q_norm, k_norm, conv_w, a_log, dt_bias, dn_out_norm,
           w_branch_a, w_branch_b, w_out, norm_ffn, w_router_group, b_router_group,
           w_router_expert, b_router_expert, w_expert_gate, w_expert_up, w_expert_down,
           norm_ple, w_ple, w_ple_gate):
    t, d = x2.shape
    qa = N_GROUPS * GROUP_WIDTH
    o_b0 = 3 * qa
    o_z = o_b0 + 3 * DN_WIDTH
    o_beta = o_z + DN_WIDTH
    o_alpha = o_beta + DN_HEADS
    o_ga = o_alpha + DN_HEADS
    o_gb = o_ga + d
    w_in = w_in.astype(F32)
    w_main = jnp.concatenate([
        w_in[:, 0:2 * qa],
        w_in[:, o_b0:o_z],
        w_in[:, o_z:o_beta],
        w_in[:, o_ga:o_gb],
        w_in[:, o_gb:o_gb + d],
        w_in[:, 2 * qa:3 * qa],
    ], axis=1).astype(BF16)
    w_ab = jnp.zeros((d, LANES), F32).at[:, 0:2 * DN_HEADS].set(w_in[:, o_beta:o_ga])
    wab_hi, wab_lo = _split2(w_ab)
    qg = jnp.tile(q_norm.astype(F32), N_GROUPS * HEADS_A) * (HEAD_DIM_A ** -0.5)
    kg = jnp.tile(k_norm.astype(F32), N_GROUPS * HEADS_A)
    qkgain = jnp.concatenate([qg, kg])[None, :]

    proj, ab, qkv_p1, qkv_p2 = _in_projection(x2, norm_mix.astype(F32)[None, :], w_main, wab_hi, wab_lo,
                                              qkgain, b, s, tm=min(1024, s))
    proj3 = proj.reshape(b, s, PROJ_WIDTH)

    cb = GROUP_WIDTH
    o0, l0 = _attention_group(proj3.reshape(b, 1, s, PROJ_WIDTH), COL_Q // cb, COL_K // cb, COL_V // cb, tq=512)
    o1, l1 = _attention_group(qkv_p1, 0, 1, 2, tq=512)
    o2, l2 = _attention_group(qkv_p2, 0, 1, 2, tq=512)

    o_b = _deltanet(proj3, ab.reshape(b, s, LANES), conv_w.astype(F32),
                    _lane_pad(a_log, DN_HEADS), _lane_pad(dt_bias, DN_HEADS),
                    dn_out_norm.astype(F32)[None, :], tile=min(512, s)).reshape(t, DN_WIDTH)

    w_r = jnp.zeros((d, LANES), F32)
    w_r = w_r.at[:, 0:N_EXPERT_GROUPS].set(w_router_group.astype(F32))
    n_exp = N_EXPERT_GROUPS * EXPERTS_PER_GROUP
    w_r = w_r.at[:, N_EXPERT_GROUPS:N_EXPERT_GROUPS + n_exp].set(w_router_expert.astype(F32))
    wr_hi, wr_lo = _split2(w_r)
    rbias = _lane_pad(jnp.concatenate([b_router_group.astype(F32), b_router_expert.astype(F32)]), 0)

    x1, h2, logits = _merge(o0.reshape(t, GROUP_WIDTH), l0.reshape(t, LANES), (o1, o2), (l1, l2), o_b,
                            proj, x2, w_branch_a.astype(BF16), w_branch_b.astype(BF16),
                            w_out.astype(BF16), norm_ffn.astype(F32)[None, :], wr_hi, wr_lo, rbias,
                            s, tm=min(512, s))

    gw = EXPERTS_PER_GROUP * D_EXPERT
    wg = jnp.transpose(w_expert_gate, (0, 2, 1, 3)).reshape(N_EXPERT_GROUPS, d, gw).astype(BF16)
    wu = jnp.transpose(w_expert_up, (0, 2, 1, 3)).reshape(N_EXPERT_GROUPS, d, gw).astype(BF16)
    wd = w_expert_down.reshape(N_EXPERT_GROUPS, gw, d).astype(BF16)
    x2n = _moe(h2, logits, x1, wg, wu, wd, tm=min(1024, t))

    return _ple(x2n, p2, norm_ple.astype(F32)[None, :], w_ple_gate.astype(BF16), w_ple.astype(BF16),
                tm=min(1024, t))


def kernel(x, p, norm_mix, w_in, q_norm, k_norm, conv_w, a_log, dt_bias, dn_out_norm, w_branch_a, w_branch_b, w_out, norm_ffn, w_router_group, b_router_group, w_router_expert, b_router_expert, w_expert_gate, w_expert_up, w_expert_down, norm_ple, w_ple, w_ple_gate):
    b, s, d = x.shape
    depth = w_in.shape[0]
    x2 = x.astype(F32).reshape(b * s, d)
    for i in range(depth):
        x2 = _layer(x2, p[i].reshape(b * s, -1), b, s, norm_mix[i], w_in[i], q_norm[i], k_norm[i],
                    conv_w[i], a_log[i], dt_bias[i], dn_out_norm[i], w_branch_a[i], w_branch_b[i],
                    w_out[i], norm_ffn[i], w_router_group[i], b_router_group[i], w_router_expert[i],
                    b_router_expert[i], w_expert_gate[i], w_expert_up[i], w_expert_down[i],
                    norm_ple[i], w_ple[i], w_ple_gate[i])
    return x2.reshape(b, s, d)
```

```python
import functools

import jax
import jax.numpy as jnp
from jax import lax
from jax.experimental import pallas as pl
from jax.experimental.pallas import tpu as pltpu

F32 = jnp.float32
BF16 = jnp.bfloat16
EPS = 1e-6

ATTN_GROUPS = ((128, 1), (512, 4), (2048, 16))
N_GROUPS = 3
HEADS_A = 8
HEAD_DIM_A = 64
GROUP_WIDTH = HEADS_A * HEAD_DIM_A
ATTN_BLOCK = 128
DN_HEADS = 8
DN_DIM = 128
DN_WIDTH = DN_HEADS * DN_DIM
CONV_WIDTH = 4
CHUNK = 64
N_EXPERT_GROUPS = 4
EXPERTS_PER_GROUP = 4
D_EXPERT = 256
LANES = 128

PROJ_TILE = 1536
COL_B, COL_Z, COL_GA, COL_GB, COL_Q, COL_K, COL_V = 0, 3072, 4096, 5120, 6144, 6656, 7168
PROJ_WIDTH = 7680
N_PLAIN_TILES = 4

VMEM_LIMIT = 56 * 1024 * 1024


def _dot(a, b):
    return jnp.dot(a, b, preferred_element_type=F32)


def _dot_nt(a, b):
    return lax.dot_general(a, b, (((1,), (1,)), ((), ())), preferred_element_type=F32)


def _dot_tn(a, b):
    return lax.dot_general(a, b, (((0,), (0,)), ((), ())), preferred_element_type=F32)


def _split2(v):
    hi = v.astype(BF16)
    lo = (v - hi.astype(F32)).astype(BF16)
    return hi, lo


def _split3(v):
    a = v.astype(BF16)
    r = v - a.astype(F32)
    b = r.astype(BF16)
    c = (r - b.astype(F32)).astype(BF16)
    return a, b, c


def _sigmoid(v):
    return 0.5 + 0.5 * jnp.tanh(0.5 * v)


def _silu(v):
    half = 0.5 * v
    return half + half * jnp.tanh(half)


def _params(sem):
    return pltpu.CompilerParams(dimension_semantics=sem, vmem_limit_bytes=VMEM_LIMIT)


def _inproj_kernel(x_ref, gain_ref, w_ref, wab_hi_ref, wab_lo_ref, qkgain_ref, red_ref, exp_ref,
                   proj_ref, ab_ref, p1_ref, p2_ref, h_scr, perm_scr, *, tm):
    j = pl.program_id(1)
    slabs = GROUP_WIDTH // LANES

    def normed(a, gain):
        ss = _dot((a * a).astype(BF16), red_ref[...])
        r = lax.rsqrt(ss * (1.0 / HEAD_DIM_A) + EPS)
        r_hi, r_lo = _split2(r)
        rexp = _dot(jnp.concatenate([r_hi, r_lo], axis=1), exp_ref[...])
        return a * rexp * gain

    def attention_tile(acc, g):
        parts = [normed(acc[:, 0:GROUP_WIDTH], qkgain_ref[:, 0:GROUP_WIDTH]),
                 normed(acc[:, GROUP_WIDTH:2 * GROUP_WIDTH], qkgain_ref[:, GROUP_WIDTH:2 * GROUP_WIDTH]),
                 acc[:, 2 * GROUP_WIDTH:3 * GROUP_WIDTH]]
        if g == 0:
            for p, val in enumerate(parts):
                proj_ref[:, p * GROUP_WIDTH:(p + 1) * GROUP_WIDTH] = val.astype(BF16)
            return
        out_ref = (p1_ref, p2_ref)[g - 1]
        dil = ATTN_GROUPS[g][1]
        for p, val in enumerate(parts):
            for c in range(slabs):
                perm_scr[c] = val[:, c * LANES:(c + 1) * LANES]
            for r in range(dil):
                for c in range(slabs):
                    lo = p * GROUP_WIDTH + c * LANES
                    out_ref[r, :, lo:lo + LANES] = (
                        perm_scr[c, pl.ds(r, tm // dil, stride=dil), :].astype(BF16))

    @pl.when(j == 0)
    def _():
        x = x_ref[...]
        ms = jnp.mean(x * x, axis=-1, keepdims=True)
        h = x * lax.rsqrt(ms + EPS) * gain_ref[...]
        hi, lo = _split2(h)
        h_scr[...] = hi
        ab_ref[...] = _dot(hi, wab_hi_ref[...]) + _dot(lo, wab_hi_ref[...]) + _dot(hi, wab_lo_ref[...])

    acc = _dot(h_scr[...], w_ref[...])

    @pl.when(j < N_PLAIN_TILES)
    def _():
        proj_ref[...] = acc.astype(BF16)

    for g in range(N_GROUPS):
        @pl.when(j == N_PLAIN_TILES + g)
        def _(g=g):
            attention_tile(acc, g)


def _in_projection(x2, norm_gain, w_main, wab_hi, wab_lo, qkgain, b, s, tm):
    t, d = x2.shape
    n_j = N_PLAIN_TILES + N_GROUPS
    assert w_main.shape[1] == n_j * PROJ_TILE and PROJ_WIDTH == (N_PLAIN_TILES + 1) * PROJ_TILE
    tps = s // tm
    col = jnp.arange(GROUP_WIDTH) // HEAD_DIM_A
    red = (col[:, None] == jnp.arange(LANES)[None, :]).astype(BF16)
    expm2 = jnp.concatenate([red.T, red.T], axis=0)
    d1, d2 = ATTN_GROUPS[1][1], ATTN_GROUPS[2][1]

    def perm_spec(dil):
        return pl.BlockSpec((None, dil, tm // dil, 3 * GROUP_WIDTH), lambda i, j: (i // tps, 0, i % tps, 0))

    return pl.pallas_call(
        functools.partial(_inproj_kernel, tm=tm),
        grid=(t // tm, n_j),
        in_specs=[
            pl.BlockSpec((tm, d), lambda i, j: (i, 0)),
            pl.BlockSpec((1, d), lambda i, j: (0, 0)),
            pl.BlockSpec((d, PROJ_TILE), lambda i, j: (0, j)),
            pl.BlockSpec((d, LANES), lambda i, j: (0, 0)),
            pl.BlockSpec((d, LANES), lambda i, j: (0, 0)),
            pl.BlockSpec((1, 2 * GROUP_WIDTH), lambda i, j: (0, 0)),
            pl.BlockSpec((GROUP_WIDTH, LANES), lambda i, j: (0, 0)),
            pl.BlockSpec((2 * LANES, GROUP_WIDTH), lambda i, j: (0, 0)),
        ],
        out_specs=[
            pl.BlockSpec((tm, PROJ_TILE), lambda i, j: (i, jnp.minimum(j, N_PLAIN_TILES))),
            pl.BlockSpec((tm, LANES), lambda i, j: (i, 0)),
            perm_spec(d1),
            perm_spec(d2),
        ],
        out_shape=[
            jax.ShapeDtypeStruct((t, PROJ_WIDTH), BF16),
            jax.ShapeDtypeStruct((t, LANES), F32),
            jax.ShapeDtypeStruct((b, d1, s // d1, 3 * GROUP_WIDTH), BF16),
            jax.ShapeDtypeStruct((b, d2, s // d2, 3 * GROUP_WIDTH), BF16),
        ],
        scratch_shapes=[pltpu.VMEM((tm, d), BF16),
                        pltpu.VMEM((GROUP_WIDTH // LANES, tm, LANES), F32)],
        compiler_params=_params(("parallel", "arbitrary")),
        name="in_proj",
    )(x2, norm_gain, w_main, wab_hi, wab_lo, qkgain, red, expm2)


def _attn_kernel(q_ref, kc_ref, kp_ref, vc_ref, vp_ref, o_ref, lse_ref, *, n_blk, pairs_at_once=4):
    n = pl.program_id(2)
    blk = ATTN_BLOCK
    qi = lax.broadcasted_iota(jnp.int32, (blk, blk), 0)
    kj = lax.broadcasted_iota(jnp.int32, (blk, blk), 1)
    cur_ok = kj <= qi
    prev_band = kj >= qi
    lane = lax.broadcasted_iota(jnp.int32, (blk, LANES), 1)
    low_half = lane < HEAD_DIM_A
    neg_inf = jnp.float32(-jnp.inf)
    ones = jnp.ones((blk, LANES), BF16)
    zero = jnp.zeros((blk, LANES), BF16)

    for b in range(n_blk):
        rows = slice(b * blk, (b + 1) * blk)
        if b == 0:
            prev_ok = jnp.logical_and(prev_band, n > 0)
        else:
            prev_ok = prev_band
        lse_tile = jnp.zeros((blk, LANES), F32)
        for p0 in range(0, HEADS_A // 2, pairs_at_once):
            pairs = range(p0, p0 + pairs_at_once)
            cols = [slice(pr * LANES, (pr + 1) * LANES) for pr in pairs]
            q2 = [q_ref[rows, c] for c in cols]
            kc2 = [kc_ref[rows, c] for c in cols]
            vc2 = [vc_ref[rows, c] for c in cols]
            if b == 0:
                kp2 = [kp_ref[:, c] for c in cols]
                vp2 = [vp_ref[:, c] for c in cols]
            else:
                prow = slice((b - 1) * blk, b * blk)
                kp2 = [kc_ref[prow, c] for c in cols]
                vp2 = [vc_ref[prow, c] for c in cols]
            qh = [jnp.where(low_half, q, zero) if half == 0 else jnp.where(low_half, zero, q)
                  for q in q2 for half in range(2)]
            kc_h = [x for x in kc2 for _ in range(2)]
            kp_h = [x for x in kp2 for _ in range(2)]
            vc_aug = [jnp.concatenate([x, ones], axis=1) for x in vc2 for _ in range(2)]
            vp_aug = [jnp.concatenate([x, ones], axis=1) for x in vp2 for _ in range(2)]
            s_c = [jnp.where(cur_ok, _dot_nt(q, kk), neg_inf) for q, kk in zip(qh, kc_h)]
            s_p = [jnp.where(prev_ok, _dot_nt(q, kk), neg_inf) for q, kk in zip(qh, kp_h)]
            m = [jnp.maximum(jnp.max(sc, axis=-1, keepdims=True), jnp.max(sp, axis=-1, keepdims=True))
                 for sc, sp in zip(s_c, s_p)]
            e_c = [jnp.exp(sc - mh).astype(BF16) for sc, mh in zip(s_c, m)]
            e_p = [jnp.exp(sp - mh).astype(BF16) for sp, mh in zip(s_p, m)]
            nd = [_dot(ec, va) + _dot(ep, vb) for ec, va, ep, vb in zip(e_c, vc_aug, e_p, vp_aug)]
            outs = [x[:, :LANES] / x[:, LANES:] for x in nd]
            for i, pr in enumerate(pairs):
                for half in range(2):
                    hh = 2 * i + half
                    lse_tile = jnp.where(lane == 2 * pr + half,
                                         m[hh] + jnp.log(nd[hh][:, LANES:LANES + 1]), lse_tile)
                o_ref[rows, cols[i]] = jnp.where(low_half, outs[2 * i], outs[2 * i + 1]).astype(BF16)
        lse_ref[rows, :] = lse_tile


def _attention_group(arr4, cq, ck, cv, tq):
    b, dil, n_sub, _ = arr4.shape
    tq = min(tq, n_sub)
    n_blk = tq // ATTN_BLOCK

    def cur(c0):
        return pl.BlockSpec((None, None, tq, GROUP_WIDTH), lambda bi, r, n: (bi, r, n, c0))

    def prev(c0):
        return pl.BlockSpec((None, None, ATTN_BLOCK, GROUP_WIDTH),
                            lambda bi, r, n: (bi, r, jnp.maximum(n * n_blk - 1, 0), c0))

    return pl.pallas_call(
        functools.partial(_attn_kernel, n_blk=n_blk),
        grid=(b, dil, n_sub // tq),
        in_specs=[cur(cq), cur(ck), prev(ck), cur(cv), prev(cv)],
        out_specs=[
            pl.BlockSpec((None, None, tq, GROUP_WIDTH), lambda bi, r, n: (bi, r, n, 0)),
            pl.BlockSpec((None, None, tq, LANES), lambda bi, r, n: (bi, r, n, 0)),
        ],
        out_shape=[
            jax.ShapeDtypeStruct((b, dil, n_sub, GROUP_WIDTH), BF16),
            jax.ShapeDtypeStruct((b, dil, n_sub, LANES), F32),
        ],
        compiler_params=_params(("parallel", "parallel", "parallel")),
        name=f"attn_d{dil}",
    )(arr4, arr4, arr4, arr4, arr4)


def _deltanet_kernel(xb_ref, z_ref, ab_ref, convw_ref, alog_ref, dtb_ref, ognorm_ref, lcum_ref,
                     o_ref, qkv_scr, gc_scr, beta_scr, carry_scr, state_scr, ext_scr,
                     u_scr, w_scr, qd_scr, kt_scr, attn_scr, *, tile, chunks_per_iter=4):
    t_idx = pl.program_id(1)
    n_chunks = tile // CHUNK
    halo = 8
    heads = range(DN_HEADS)

    @pl.when(t_idx == 0)
    def _():
        carry_scr[...] = jnp.zeros_like(carry_scr)
        state_scr[...] = jnp.zeros_like(state_scr)

    for c in range(3 * DN_HEADS):
        cols = slice(c * DN_DIM, (c + 1) * DN_DIM)
        cur = xb_ref[:, cols].astype(F32)
        ext_scr[c, 0:halo, :] = carry_scr[:, cols]
        ext_scr[c, halo:halo + tile, :] = cur
        y = cur * convw_ref[CONV_WIDTH - 1:CONV_WIDTH, cols]
        for j in range(CONV_WIDTH - 1):
            off = halo - (CONV_WIDTH - 1) + j
            y = y + ext_scr[c, pl.ds(off, tile), :] * convw_ref[j:j + 1, cols]
        carry_scr[:, cols] = cur[tile - halo:tile]
        y = _silu(y)
        if c < 2 * DN_HEADS:
            y = y * lax.rsqrt(jnp.sum(y * y, axis=-1, keepdims=True) + EPS)
            if c < DN_HEADS:
                y = y * (DN_DIM ** -0.5)
        qkv_scr[:, cols] = y

    ab = ab_ref[...]
    beta_scr[...] = _sigmoid(ab)
    sp_in = ab + dtb_ref[...]
    softplus = jnp.maximum(sp_in, 0.0) + jnp.log(1.0 + jnp.exp(-jnp.abs(sp_in)))
    gdec = -jnp.exp(alog_ref[...]) * softplus
    g1, g2, g3 = _split3(gdec)
    lcum = lcum_ref[...]
    gc_scr[...] = _dot(lcum, g1) + _dot(lcum, g2) + _dot(lcum, g3)

    ri = lax.broadcasted_iota(jnp.int32, (CHUNK, CHUNK), 0)
    ci = lax.broadcasted_iota(jnp.int32, (CHUNK, CHUNK), 1)
    tril = ci <= ri
    tril_strict = ci < ri
    eye = (ci == ri).astype(F32)
    neg_inf = jnp.float32(-jnp.inf)

    def intra_body(c, carry):
        rows_c = [pl.ds(pl.multiple_of((c * chunks_per_iter + i) * CHUNK, CHUNK), CHUNK)
                  for i in range(chunks_per_iter)]
        units = [(i, h) for i in range(chunks_per_iter) for h in heads]
        gc_c = [gc_scr[r, :] for r in rows_c]
        gc_t_c = [jnp.transpose(g) for g in gc_c]
        beta_c = [beta_scr[r, :] for r in rows_c]
        q = [qkv_scr[rows_c[i], h * DN_DIM:(h + 1) * DN_DIM] for i, h in units]
        k = [qkv_scr[rows_c[i], DN_WIDTH + h * DN_DIM:DN_WIDTH + (h + 1) * DN_DIM] for i, h in units]
        v = [qkv_scr[rows_c[i], 2 * DN_WIDTH + h * DN_DIM:2 * DN_WIDTH + (h + 1) * DN_DIM] for i, h in units]

        b_col = [beta_c[i][:, h:h + 1] for i, h in units]
        g_col = [gc_c[i][:, DN_HEADS + h:DN_HEADS + h + 1] for i, h in units]
        g_row = [gc_t_c[i][DN_HEADS + h:DN_HEADS + h + 1, :] for i, h in units]
        g_last = [gcol[CHUNK - 1:CHUNK, :] for gcol in g_col]
        decay = [jnp.exp(jnp.where(tril, gcol - grow, neg_inf)) for gcol, grow in zip(g_col, g_row)]
        e_g = [jnp.exp(gcol) for gcol in g_col]
        kb = [kh * bh for kh, bh in zip(k, b_col)]
        k16 = [kh.astype(BF16) for kh in k]
        a_mat = [jnp.where(tril_strict, _dot_nt(kbh.astype(BF16), kh16) * dh, 0.0)
                 for kbh, kh16, dh in zip(kb, k16, decay)]
        pw = [-a for a in a_mat]
        t_mat = [eye + p for p in pw]
        for _ in range(5):
            pw16 = [p.astype(BF16) for p in pw]
            pw = [_dot(p, p) for p in pw16]
            t_mat = [th + _dot(th.astype(BF16), p.astype(BF16)) for th, p in zip(t_mat, pw)]
        t16 = [th.astype(BF16) for th in t_mat]
        u = [_dot(th, (vh * bh).astype(BF16)) for th, vh, bh in zip(t16, v, b_col)]
        w = [_dot(th, (kbh * eh).astype(BF16)) for th, kbh, eh in zip(t16, kb, e_g)]
        attn = [jnp.where(tril, _dot_nt(qh.astype(BF16), kh16) * dh, 0.0).astype(BF16)
                for qh, kh16, dh in zip(q, k16, decay)]
        q_dec = [(qh * eh).astype(BF16) for qh, eh in zip(q, e_g)]
        k_tail = [(kh * jnp.exp(gl - gcol)).astype(BF16) for kh, gl, gcol in zip(k, g_last, g_col)]
        for n, (i, h) in enumerate(units):
            hc = slice(h * DN_DIM, (h + 1) * DN_DIM)
            u_scr[rows_c[i], hc] = u[n]
            w_scr[rows_c[i], hc] = w[n].astype(BF16)
            qd_scr[rows_c[i], hc] = q_dec[n]
            kt_scr[rows_c[i], hc] = k_tail[n]
            attn_scr[h, rows_c[i], :] = attn[n]
        return carry

    lax.fori_loop(0, n_chunks // chunks_per_iter, intra_body, 0)

    def scan_body(c, carry):
        r0 = pl.multiple_of(c * CHUNK, CHUNK)
        rows = pl.ds(r0, CHUNK)
        decay_last = jnp.exp(gc_scr[pl.ds(r0 + CHUNK - 1, 1), :])
        hcs = [slice(h * DN_DIM, (h + 1) * DN_DIM) for h in heads]
        state = [state_scr[h] for h in heads]
        u = [u_scr[rows, hc] for hc in hcs]
        w16 = [w_scr[rows, hc] for hc in hcs]
        qd = [qd_scr[rows, hc] for hc in hcs]
        kt = [kt_scr[rows, hc] for hc in hcs]
        attn = [attn_scr[h, rows, :] for h in heads]
        z = [z_ref[rows, hc] for hc in hcs]
        s16 = [sh.astype(BF16) for sh in state]
        v_new = [(uh - _dot(wh, sh)).astype(BF16) for uh, wh, sh in zip(u, w16, s16)]
        o = [_dot(qh, sh) + _dot(ah, vn) for qh, sh, ah, vn in zip(qd, s16, attn, v_new)]
        new_state = [sh * decay_last[:, DN_HEADS + h:DN_HEADS + h + 1] + _dot_tn(kth, vn)
                     for h, sh, kth, vn in zip(heads, state, kt, v_new)]
        o = [oh * lax.rsqrt(jnp.mean(oh * oh, axis=-1, keepdims=True) + EPS) * ognorm_ref[...] for oh in o]
        o = [(oh * _silu(zh.astype(F32))).astype(BF16) for oh, zh in zip(o, z)]
        for h in heads:
            state_scr[h] = new_state[h]
        for h in heads:
            o_ref[rows, hcs[h]] = o[h]
        return carry

    lax.fori_loop(0, n_chunks, scan_body, 0)


def _deltanet(proj3, ab3, conv_w, a_log_l, dtb_l, og_norm, tile):
    b, s, w = proj3.shape
    lcum = jnp.tril(jnp.ones((tile, tile), F32))
    same_chunk = (jnp.arange(tile)[:, None] // CHUNK) == (jnp.arange(tile)[None, :] // CHUNK)
    lcum = jnp.where(same_chunk, lcum, 0.0).astype(BF16)
    return pl.pallas_call(
        functools.partial(_deltanet_kernel, tile=tile),
        grid=(b, s // tile),
        in_specs=[
            pl.BlockSpec((None, tile, 3 * DN_WIDTH), lambda bi, t: (bi, t, COL_B // (3 * DN_WIDTH))),
            pl.BlockSpec((None, tile, DN_WIDTH), lambda bi, t: (bi, t, COL_Z // DN_WIDTH)),
            pl.BlockSpec((None, tile, LANES), lambda bi, t: (bi, t, 0)),
            pl.BlockSpec((CONV_WIDTH, 3 * DN_WIDTH), lambda bi, t: (0, 0)),
            pl.BlockSpec((1, LANES), lambda bi, t: (0, 0)),
            pl.BlockSpec((1, LANES), lambda bi, t: (0, 0)),
            pl.BlockSpec((1, DN_DIM), lambda bi, t: (0, 0)),
            pl.BlockSpec((tile, tile), lambda bi, t: (0, 0)),
        ],
        out_specs=pl.BlockSpec((None, tile, DN_WIDTH), lambda bi, t: (bi, t, 0)),
        out_shape=jax.ShapeDtypeStruct((b, s, DN_WIDTH), BF16),
        scratch_shapes=[
            pltpu.VMEM((tile, 3 * DN_WIDTH), F32),
            pltpu.VMEM((tile, LANES), F32),
            pltpu.VMEM((tile, LANES), F32),
            pltpu.VMEM((8, 3 * DN_WIDTH), F32),
            pltpu.VMEM((DN_HEADS, DN_DIM, DN_DIM), F32),
            pltpu.VMEM((3 * DN_HEADS, 8 + tile, DN_DIM), F32),
            pltpu.VMEM((tile, DN_WIDTH), F32),
            pltpu.VMEM((tile, DN_WIDTH), BF16),
            pltpu.VMEM((tile, DN_WIDTH), BF16),
            pltpu.VMEM((tile, DN_WIDTH), BF16),
            pltpu.VMEM((DN_HEADS, tile, CHUNK), BF16),
        ],
        compiler_params=_params(("parallel", "arbitrary")),
        name="deltanet",
    )(proj3, proj3, ab3, conv_w, a_log_l, dtb_l, og_norm, lcum)


def _merge_kernel(o0_ref, o1_ref, o2_ref, l0_ref, l1_ref, l2_ref, ob_ref, ga_ref, gb_ref, x_ref,
                  wa_ref, wb_ref, wo_ref, hexp_ref, gain_ref, wr_hi_ref, wr_lo_ref, rbias_ref,
                  x1_ref, h2_ref, logit_ref, o1_scr, o2_scr, l1_scr, l2_scr, *, tm):
    for g, src, dst in ((1, o1_ref, o1_scr), (2, o2_ref, o2_scr), (1, l1_ref, l1_scr), (2, l2_ref, l2_scr)):
        dil = ATTN_GROUPS[g][1]
        for r in range(dil):
            for c in range(dst.shape[0]):
                dst[c, pl.ds(r, tm // dil, stride=dil), :] = src[r, :, c * LANES:(c + 1) * LANES].astype(F32)

    def slabs(scr):
        return jnp.concatenate([scr[c] for c in range(scr.shape[0])], axis=-1)

    l0, l1, l2 = l0_ref[...], l1_scr[0], l2_scr[0]
    m = jnp.maximum(jnp.maximum(l0, l1), l2)
    e0, e1, e2 = jnp.exp(l0 - m), jnp.exp(l1 - m), jnp.exp(l2 - m)
    tot = e0 + e1 + e2
    hexp = hexp_ref[...]
    o_a = jnp.zeros(o0_ref.shape, F32)
    for e, o_val in ((e0, o0_ref[...].astype(F32)), (e1, slabs(o1_scr)), (e2, slabs(o2_scr))):
        w_hi, w_lo = _split2(e / tot)
        o_a = o_a + (_dot(w_hi, hexp) + _dot(w_lo, hexp)) * o_val
    ya = _dot(o_a.astype(BF16), wa_ref[...])
    yb = _dot(ob_ref[...], wb_ref[...])
    merged = _sigmoid(ga_ref[...].astype(F32)) * ya + _sigmoid(gb_ref[...].astype(F32)) * yb
    x1 = x_ref[...] + _dot(merged.astype(BF16), wo_ref[...])
    x1_ref[...] = x1
    h2 = x1 * lax.rsqrt(jnp.mean(x1 * x1, axis=-1, keepdims=True) + EPS) * gain_ref[...]
    hi, lo = _split2(h2)
    h2_ref[...] = hi
    logit_ref[...] = (_dot(hi, wr_hi_ref[...]) + _dot(lo, wr_hi_ref[...]) + _dot(hi, wr_lo_ref[...])
                      + rbias_ref[...])


def _merge(o0, l0, o_perm, l_perm, o_b, proj, x2, wa, wb, wo, gain, wr_hi, wr_lo, rbias, s, tm):
    t, d = x2.shape
    tps = s // tm
    head_of_col = jnp.arange(GROUP_WIDTH) // HEAD_DIM_A
    hexp = (jnp.arange(LANES)[:, None] == head_of_col[None, :]).astype(BF16)

    def row(wd, cblk=0):
        return pl.BlockSpec((tm, wd), lambda i: (i, cblk))

    def perm(arr):
        _, dil, _, wd = arr.shape
        return pl.BlockSpec((None, dil, tm // dil, wd), lambda i: (i // tps, 0, i % tps, 0))

    def full(a):
        return pl.BlockSpec(a.shape, lambda i: (0, 0))

    return pl.pallas_call(
        functools.partial(_merge_kernel, tm=tm),
        grid=(t // tm,),
        in_specs=[row(GROUP_WIDTH), perm(o_perm[0]), perm(o_perm[1]),
                  row(LANES), perm(l_perm[0]), perm(l_perm[1]),
                  row(DN_WIDTH), row(d, COL_GA // d), row(d, COL_GB // d), row(d),
                  full(wa), full(wb), full(wo), full(hexp), full(gain), full(wr_hi), full(wr_lo),
                  full(rbias)],
        out_specs=[row(d), row(d), row(LANES)],
        out_shape=[
            jax.ShapeDtypeStruct((t, d), F32),
            jax.ShapeDtypeStruct((t, d), BF16),
            jax.ShapeDtypeStruct((t, LANES), F32),
        ],
        scratch_shapes=[pltpu.VMEM((GROUP_WIDTH // LANES, tm, LANES), F32),
                        pltpu.VMEM((GROUP_WIDTH // LANES, tm, LANES), F32),
                        pltpu.VMEM((1, tm, LANES), F32), pltpu.VMEM((1, tm, LANES), F32)],
        compiler_params=_params(("parallel",)),
        name="merge",
    )(o0, o_perm[0], o_perm[1], l0, l_perm[0], l_perm[1], o_b, proj, proj, x2,
      wa, wb, wo, hexp, gain, wr_hi, wr_lo, rbias)


def _routing_weights(logits):
    lane_i = lax.broadcasted_iota(jnp.int32, logits.shape, 1)
    lane = lane_i.astype(F32)
    neg_inf = jnp.float32(-jnp.inf)
    big = jnp.float32(1 << 20)
    is_group = lane_i < N_EXPERT_GROUPS
    gl = jnp.where(is_group, logits, neg_inf)
    gmax = jnp.max(gl, axis=-1, keepdims=True)
    gsum = jnp.sum(jnp.exp(gl - gmax), axis=-1, keepdims=True)
    p_g = 1.0 / gsum
    g_idx = jnp.min(jnp.where(gl == gmax, lane, big), axis=-1, keepdims=True)
    n_exp = N_EXPERT_GROUPS * EXPERTS_PER_GROUP
    is_exp = jnp.logical_and(lane_i >= N_EXPERT_GROUPS, lane_i < N_EXPERT_GROUPS + n_exp)
    exp_group = jnp.right_shift(lane_i - N_EXPERT_GROUPS, 2).astype(F32)
    assert EXPERTS_PER_GROUP == 4
    sel = jnp.logical_and(is_exp, exp_group == g_idx)
    el = jnp.where(sel, logits, neg_inf)
    emax = jnp.max(el, axis=-1, keepdims=True)
    ee = jnp.exp(el - emax)
    esum = jnp.sum(ee, axis=-1, keepdims=True)
    idx1 = jnp.min(jnp.where(el == emax, lane, big), axis=-1, keepdims=True)
    el2 = jnp.where(lane == idx1, neg_inf, el)
    e2max = jnp.max(el2, axis=-1, keepdims=True)
    idx2 = jnp.min(jnp.where(el2 == e2max, lane, big), axis=-1, keepdims=True)
    p1 = 1.0 / esum
    p2 = jnp.exp(e2max - emax) / esum
    top_sum = p1 + p2
    w = jnp.where(lane == idx1, p1 / top_sum, jnp.where(lane == idx2, p2 / top_sum, 0.0))
    return w * p_g


def _moe_kernel(h2_ref, logit_ref, x1_ref, wg_ref, wu_ref, wd_ref, out_ref, cw_scr, acc_scr):
    g = pl.program_id(1)

    @pl.when(g == 0)
    def _():
        cw_scr[...] = _routing_weights(logit_ref[...])
        acc_scr[...] = jnp.zeros_like(acc_scr)

    h2 = h2_ref[...]
    hg = _dot(h2, wg_ref[...])
    hu = _dot(h2, wu_ref[...])
    cw = cw_scr[...]
    lane = lax.broadcasted_iota(jnp.int32, cw.shape, 1)
    parts = []
    for j in range(EXPERTS_PER_GROUP):
        cols = slice(j * D_EXPERT, (j + 1) * D_EXPERT)
        want = N_EXPERT_GROUPS + g * EXPERTS_PER_GROUP + j
        c_col = jnp.sum(jnp.where(lane == want, cw, 0.0), axis=-1, keepdims=True)
        parts.append((_silu(hg[:, cols]) * hu[:, cols] * c_col).astype(BF16))
    act = jnp.concatenate(parts, axis=-1)
    acc_scr[...] += _dot(act, wd_ref[...])

    @pl.when(g == N_EXPERT_GROUPS - 1)
    def _():
        out_ref[...] = x1_ref[...] + acc_scr[...]


def _moe(h2, logits, x1, wg, wu, wd, tm):
    t, d = x1.shape
    gw = EXPERTS_PER_GROUP * D_EXPERT
    return pl.pallas_call(
        _moe_kernel,
        grid=(t // tm, N_EXPERT_GROUPS),
        in_specs=[
            pl.BlockSpec((tm, d), lambda i, g: (i, 0)),
            pl.BlockSpec((tm, LANES), lambda i, g: (i, 0)),
            pl.BlockSpec((tm, d), lambda i, g: (i, 0)),
            pl.BlockSpec((None, d, gw), lambda i, g: (g, 0, 0)),
            pl.BlockSpec((None, d, gw), lambda i, g: (g, 0, 0)),
            pl.BlockSpec((None, gw, d), lambda i, g: (g, 0, 0)),
        ],
        out_specs=pl.BlockSpec((tm, d), lambda i, g: (i, 0)),
        out_shape=jax.ShapeDtypeStruct((t, d), F32),
        scratch_shapes=[pltpu.VMEM((tm, LANES), F32), pltpu.VMEM((tm, d), F32)],
        compiler_params=_params(("parallel", "arbitrary")),
        name="moe",
    )(h2, logits, x1, wg, wu, wd)


def _ple_kernel(x_ref, p_ref, gain_ref, wpg_ref, wple_ref, out_ref):
    x = x_ref[...]
    h3 = x * lax.rsqrt(jnp.mean(x * x, axis=-1, keepdims=True) + EPS) * gain_ref[...]
    gate = _sigmoid(_dot(h3.astype(BF16), wpg_ref[...]))
    ple = _dot(p_ref[...].astype(BF16), wple_ref[...])
    out_ref[...] = x + gate * ple


def _ple(x2, p2, gain, wpg, wple, tm):
    t, d = x2.shape
    pd = p2.shape[1]
    return pl.pallas_call(
        _ple_kernel,
        grid=(t // tm,),
        in_specs=[
            pl.BlockSpec((tm, d), lambda i: (i, 0)),
            pl.BlockSpec((tm, pd), lambda i: (i, 0)),
            pl.BlockSpec((1, d), lambda i: (0, 0)),
            pl.BlockSpec((d, d), lambda i: (0, 0)),
            pl.BlockSpec((pd, d), lambda i: (0, 0)),
        ],
        out_specs=pl.BlockSpec((tm, d), lambda i: (i, 0)),
        out_shape=jax.ShapeDtypeStruct((t, d), F32),
        compiler_params=_params(("parallel",)),
        name="ple",
    )(x2, p2, gain, wpg, wple)


def _lane_pad(v, offset):
    out = jnp.zeros((1, LANES), F32)
    return out.at[0, offset:offset + v.shape[0]].set(v.astype(F32))


def _layer(x2, p2, b, s, norm_mix, w_in, q_norm, k_norm, conv_w, a_log, dt_bias, dn_out_norm,
           w_branch_a, w_branch_b, w_out, norm_ffn, w_router_group, b_router_group,
           w_router_expert, b_router_expert, w_expert_gate, w_expert_up, w_expert_down,
           norm_ple, w_ple, w_ple_gate):
    t, d = x2.shape
    qa = N_GROUPS * GROUP_WIDTH
    o_b0 = 3 * qa
    o_z = o_b0 + 3 * DN_WIDTH
    o_beta = o_z + DN_WIDTH
    o_alpha = o_beta + DN_HEADS
    o_ga = o_alpha + DN_HEADS
    o_gb = o_ga + d
    w_in = w_in.astype(F32)
    gw_a = GROUP_WIDTH
    group_tiles = [jnp.concatenate([w_in[:, part * qa + g * gw_a:part * qa + (g + 1) * gw_a]
                                    for part in range(3)], axis=1) for g in range(N_GROUPS)]
    w_main = jnp.concatenate([
        w_in[:, o_b0:o_z],
        w_in[:, o_z:o_beta],
        w_in[:, o_ga:o_gb],
        w_in[:, o_gb:o_gb + d],
    ] + group_tiles, axis=1).astype(BF16)
    w_ab = jnp.zeros((d, LANES), F32).at[:, 0:2 * DN_HEADS].set(w_in[:, o_beta:o_ga])
    wab_hi, wab_lo = _split2(w_ab)
    qg = jnp.tile(q_norm.astype(F32), HEADS_A) * (HEAD_DIM_A ** -0.5)
    kg = jnp.tile(k_norm.astype(F32), HEADS_A)
    qkgain = jnp.concatenate([qg, kg])[None, :]

    proj, ab, qkv_p1, qkv_p2 = _in_projection(x2, norm_mix.astype(F32)[None, :], w_main, wab_hi, wab_lo,
                                              qkgain, b, s, tm=min(1024, s))
    proj3 = proj.reshape(b, s, PROJ_WIDTH)

    cb = GROUP_WIDTH
    o0, l0 = _attention_group(proj3.reshape(b, 1, s, PROJ_WIDTH), COL_Q // cb, COL_K // cb, COL_V // cb, tq=512)
    o1, l1 = _attention_group(qkv_p1, 0, 1, 2, tq=512)
    o2, l2 = _attention_group(qkv_p2, 0, 1, 2, tq=512)

    o_b = _deltanet(proj3, ab.reshape(b, s, LANES), conv_w.astype(F32),
                    _lane_pad(a_log, DN_HEADS), _lane_pad(dt_bias, DN_HEADS),
                    dn_out_norm.astype(F32)[None, :], tile=min(512, s)).reshape(t, DN_WIDTH)

    w_r = jnp.zeros((d, LANES), F32)
    w_r = w_r.at[:, 0:N_EXPERT_GROUPS].set(w_router_group.astype(F32))
    n_exp = N_EXPERT_GROUPS * EXPERTS_PER_GROUP
    w_r = w_r.at[:, N_EXPERT_GROUPS:N_EXPERT_GROUPS + n_exp].set(w_router_expert.astype(F32))
    wr_hi, wr_lo = _split2(w_r)
    rbias = _lane_pad(jnp.concatenate([b_router_group.astype(F32), b_router_expert.astype(F32)]), 0)

    x1, h2, logits = _merge(o0.reshape(t, GROUP_WIDTH), l0.reshape(t, LANES), (o1, o2), (l1, l2), o_b,
                            proj, x2, w_branch_a.astype(BF16), w_branch_b.astype(BF16),
                            w_out.astype(BF16), norm_ffn.astype(F32)[None, :], wr_hi, wr_lo, rbias,
                            s, tm=min(512, s))

    gw = EXPERTS_PER_GROUP * D_EXPERT
    wg = jnp.transpose(w_expert_gate, (0, 2, 1, 3)).reshape(N_EXPERT_GROUPS, d, gw).astype(BF16)
    wu = jnp.transpose(w_expert_up, (0, 2, 1, 3)).reshape(N_EXPERT_GROUPS, d, gw).astype(BF16)
    wd = w_expert_down.reshape(N_EXPERT_GROUPS, gw, d).astype(BF16)
    x2n = _moe(h2, logits, x1, wg, wu, wd, tm=min(1024, t))

    return _ple(x2n, p2, norm_ple.astype(F32)[None, :], w_ple_gate.astype(BF16), w_ple.astype(BF16),
                tm=min(1024, t))


def kernel(x, p, norm_mix, w_in, q_norm, k_norm, conv_w, a_log, dt_bias, dn_out_norm, w_branch_a, w_branch_b, w_out, norm_ffn, w_router_group, b_router_group, w_router_expert, b_router_expert, w_expert_gate, w_expert_up, w_expert_down, norm_ple, w_ple, w_ple_gate):
    b, s, d = x.shape
    depth = w_in.shape[0]
    x2 = x.astype(F32).reshape(b * s, d)
    for i in range(depth):
        x2 = _layer(x2, p[i].reshape(b * s, -1), b, s, norm_mix[i], w_in[i], q_norm[i], k_norm[i],
                    conv_w[i], a_log[i], dt_bias[i], dn_out_norm[i], w_branch_a[i], w_branch_b[i],
                    w_out[i], norm_ffn[i], w_router_group[i], b_router_group[i], w_router_expert[i],
                    b_router_expert[i], w_expert_gate[i], w_expert_up[i], w_expert_down[i],
                    norm_ple[i], w_ple[i], w_ple_gate[i])
    return x2.reshape(b, s, d)
```

```python
import functools

import jax
import jax.numpy as jnp
from jax import lax
from jax.experimental import pallas as pl
from jax.experimental.pallas import tpu as pltpu

F32 = jnp.float32
BF16 = jnp.bfloat16
EPS = 1e-6
LN2 = 0.6931471805599453
LOG2E = 1.4426950408889634

ATTN_GROUPS = ((128, 1), (512, 4), (2048, 16))
N_GROUPS = 3
HEADS_A = 8
HEAD_DIM_A = 64
GROUP_WIDTH = HEADS_A * HEAD_DIM_A
ATTN_BLOCK = 128
DN_HEADS = 8
DN_DIM = 128
DN_WIDTH = DN_HEADS * DN_DIM
CONV_WIDTH = 4
CHUNK = 64
N_EXPERT_GROUPS = 4
EXPERTS_PER_GROUP = 4
D_EXPERT = 256
SUBLANES = 8
LANES = 128

PROJ_TILE = 1536
COL_B, COL_Z, COL_GA, COL_GB, COL_Q, COL_K, COL_V = 0, 3072, 4096, 5120, 6144, 6656, 7168
PROJ_WIDTH = 7680
N_PLAIN_TILES = 4

VMEM_LIMIT = 56 * 1024 * 1024


def _dot(a, b):
    return jnp.dot(a, b, preferred_element_type=F32)


def _dot_nt(a, b):
    return lax.dot_general(a, b, (((1,), (1,)), ((), ())), preferred_element_type=F32)


def _dot_tn(a, b):
    return lax.dot_general(a, b, (((0,), (0,)), ((), ())), preferred_element_type=F32)


def _split2(v):
    hi = v.astype(BF16)
    lo = (v - hi.astype(F32)).astype(BF16)
    return hi, lo


def _split3(v):
    a = v.astype(BF16)
    r = v - a.astype(F32)
    b = r.astype(BF16)
    c = (r - b.astype(F32)).astype(BF16)
    return a, b, c


def _sigmoid(v):
    return 0.5 + 0.5 * jnp.tanh(0.5 * v)


def _silu(v):
    half = 0.5 * v
    return half + half * jnp.tanh(half)


def _params(sem):
    return pltpu.CompilerParams(dimension_semantics=sem, vmem_limit_bytes=VMEM_LIMIT)


def _inproj_kernel(x_ref, gain_ref, w_ref, wab_hi_ref, wab_lo_ref, qkgain_ref, red_ref, exp_ref,
                   proj_ref, ab_ref, p1_ref, p2_ref, h_scr, perm_scr, *, tm):
    j = pl.program_id(1)
    slabs = GROUP_WIDTH // LANES

    def normed(a, gain):
        ss = _dot((a * a).astype(BF16), red_ref[...])
        r = lax.rsqrt(ss * (1.0 / HEAD_DIM_A) + EPS)
        r_hi, r_lo = _split2(r)
        rexp = _dot(jnp.concatenate([r_hi, r_lo], axis=1), exp_ref[...])
        return a * rexp * gain

    def attention_tile(acc, g):
        parts = [normed(acc[:, 0:GROUP_WIDTH], qkgain_ref[:, 0:GROUP_WIDTH]),
                 normed(acc[:, GROUP_WIDTH:2 * GROUP_WIDTH], qkgain_ref[:, GROUP_WIDTH:2 * GROUP_WIDTH]),
                 acc[:, 2 * GROUP_WIDTH:3 * GROUP_WIDTH]]
        if g == 0:
            for p, val in enumerate(parts):
                proj_ref[:, p * GROUP_WIDTH:(p + 1) * GROUP_WIDTH] = val.astype(BF16)
            return
        out_ref = (p1_ref, p2_ref)[g - 1]
        dil = ATTN_GROUPS[g][1]
        n_u = tm // dil
        for p, val in enumerate(parts):
            if dil < SUBLANES:
                for c in range(slabs):
                    perm_scr[c, 0:tm, :] = val[:, c * LANES:(c + 1) * LANES]
                for r in range(dil):
                    for c in range(slabs):
                        lo = p * GROUP_WIDTH + c * LANES
                        out_ref[r, :, lo:lo + LANES] = (
                            perm_scr[c, pl.ds(r, n_u, stride=dil), :].astype(BF16))
            else:
                pitch = n_u + 1
                for c in range(slabs):
                    for m in range(tm // SUBLANES):
                        u, r0 = divmod(m * SUBLANES, dil)
                        perm_scr[c, pl.ds(r0 * pitch + u, SUBLANES, stride=pitch), :] = (
                            val[m * SUBLANES:(m + 1) * SUBLANES, c * LANES:(c + 1) * LANES])
                for r in range(dil):
                    for c in range(slabs):
                        lo = p * GROUP_WIDTH + c * LANES
                        out_ref[r, :, lo:lo + LANES] = perm_scr[c, pl.ds(r * pitch, n_u), :].astype(BF16)

    @pl.when(j == 0)
    def _():
        x = x_ref[...]
        ms = jnp.mean(x * x, axis=-1, keepdims=True)
        h = x * lax.rsqrt(ms + EPS) * gain_ref[...]
        hi, lo = _split2(h)
        h_scr[...] = hi
        ab_ref[...] = _dot(hi, wab_hi_ref[...]) + _dot(lo, wab_hi_ref[...]) + _dot(hi, wab_lo_ref[...])

    acc = _dot(h_scr[...], w_ref[...])

    @pl.when(j < N_PLAIN_TILES)
    def _():
        proj_ref[...] = acc.astype(BF16)

    for g in range(N_GROUPS):
        @pl.when(j == N_PLAIN_TILES + g)
        def _(g=g):
            attention_tile(acc, g)


def _in_projection(x2, norm_gain, w_main, wab_hi, wab_lo, qkgain, b, s, tm):
    t, d = x2.shape
    n_j = N_PLAIN_TILES + N_GROUPS
    assert w_main.shape[1] == n_j * PROJ_TILE and PROJ_WIDTH == (N_PLAIN_TILES + 1) * PROJ_TILE
    tps = s // tm
    col = jnp.arange(GROUP_WIDTH) // HEAD_DIM_A
    red = (col[:, None] == jnp.arange(LANES)[None, :]).astype(BF16)
    expm2 = jnp.concatenate([red.T, red.T], axis=0)
    d1, d2 = ATTN_GROUPS[1][1], ATTN_GROUPS[2][1]

    def perm_spec(dil):
        return pl.BlockSpec((None, dil, tm // dil, 3 * GROUP_WIDTH), lambda i, j: (i // tps, 0, i % tps, 0))

    return pl.pallas_call(
        functools.partial(_inproj_kernel, tm=tm),
        grid=(t // tm, n_j),
        in_specs=[
            pl.BlockSpec((tm, d), lambda i, j: (i, 0)),
            pl.BlockSpec((1, d), lambda i, j: (0, 0)),
            pl.BlockSpec((d, PROJ_TILE), lambda i, j: (0, j)),
            pl.BlockSpec((d, LANES), lambda i, j: (0, 0)),
            pl.BlockSpec((d, LANES), lambda i, j: (0, 0)),
            pl.BlockSpec((1, 2 * GROUP_WIDTH), lambda i, j: (0, 0)),
            pl.BlockSpec((GROUP_WIDTH, LANES), lambda i, j: (0, 0)),
            pl.BlockSpec((2 * LANES, GROUP_WIDTH), lambda i, j: (0, 0)),
        ],
        out_specs=[
            pl.BlockSpec((tm, PROJ_TILE), lambda i, j: (i, jnp.minimum(j, N_PLAIN_TILES))),
            pl.BlockSpec((tm, LANES), lambda i, j: (i, 0)),
            perm_spec(d1),
            perm_spec(d2),
        ],
        out_shape=[
            jax.ShapeDtypeStruct((t, PROJ_WIDTH), BF16),
            jax.ShapeDtypeStruct((t, LANES), F32),
            jax.ShapeDtypeStruct((b, d1, s // d1, 3 * GROUP_WIDTH), BF16),
            jax.ShapeDtypeStruct((b, d2, s // d2, 3 * GROUP_WIDTH), BF16),
        ],
        scratch_shapes=[pltpu.VMEM((tm, d), BF16),
                        pltpu.VMEM((GROUP_WIDTH // LANES, tm + max(d1, d2), LANES), F32)],
        compiler_params=_params(("parallel", "arbitrary")),
        name="in_proj",
    )(x2, norm_gain, w_main, wab_hi, wab_lo, qkgain, red, expm2)


def _attn_kernel(q_ref, kc_ref, kp_ref, vc_ref, vp_ref, o_ref, lse_ref, *, n_blk, pairs_at_once=4):
    n = pl.program_id(2)
    blk = ATTN_BLOCK
    qi = lax.broadcasted_iota(jnp.int32, (2 * blk, blk), 0) & (blk - 1)
    kj = lax.broadcasted_iota(jnp.int32, (2 * blk, blk), 1)
    cur_ok2 = kj <= qi
    prev_band = kj >= qi
    lane = lax.broadcasted_iota(jnp.int32, (blk, LANES), 1)
    low_half = lane < HEAD_DIM_A
    neg_inf = jnp.float32(-jnp.inf)
    ones = jnp.ones((blk, LANES), BF16)
    zero = jnp.zeros((blk, LANES), BF16)

    for b in range(n_blk):
        rows = slice(b * blk, (b + 1) * blk)
        if b == 0:
            prev_ok2 = jnp.logical_and(prev_band, n > 0)
        else:
            prev_ok2 = prev_band
        lse_tile = jnp.zeros((blk, LANES), F32)
        for p0 in range(0, HEADS_A // 2, pairs_at_once):
            pairs = range(p0, p0 + pairs_at_once)
            cols = [slice(pr * LANES, (pr + 1) * LANES) for pr in pairs]
            q2 = [q_ref[rows, c] for c in cols]
            kc2 = [kc_ref[rows, c] for c in cols]
            vc2 = [vc_ref[rows, c] for c in cols]
            if b == 0:
                kp2 = [kp_ref[:, c] for c in cols]
                vp2 = [vp_ref[:, c] for c in cols]
            else:
                prow = slice((b - 1) * blk, b * blk)
                kp2 = [kc_ref[prow, c] for c in cols]
                vp2 = [vc_ref[prow, c] for c in cols]
            q_st = [jnp.concatenate([jnp.where(low_half, q, zero), jnp.where(low_half, zero, q)], axis=0)
                    for q in q2]
            vc_aug = [jnp.concatenate([x, ones], axis=1) for x in vc2]
            vp_aug = [jnp.concatenate([x, ones], axis=1) for x in vp2]
            s_c = [jnp.where(cur_ok2, _dot_nt(q, kk), neg_inf) for q, kk in zip(q_st, kc2)]
            s_p = [jnp.where(prev_ok2, _dot_nt(q, kk), neg_inf) for q, kk in zip(q_st, kp2)]
            m = [jnp.maximum(jnp.max(sc, axis=-1, keepdims=True), jnp.max(sp, axis=-1, keepdims=True))
                 for sc, sp in zip(s_c, s_p)]
            e_c = [jnp.exp2(sc - mh).astype(BF16) for sc, mh in zip(s_c, m)]
            e_p = [jnp.exp2(sp - mh).astype(BF16) for sp, mh in zip(s_p, m)]
            nd = [_dot(ec, va) + _dot(ep, vb) for ec, va, ep, vb in zip(e_c, vc_aug, e_p, vp_aug)]
            outs = [x[:, :LANES] / x[:, LANES:] for x in nd]
            lse = [mh * LN2 + jnp.log(x[:, LANES:LANES + 1]) for mh, x in zip(m, nd)]
            for i, pr in enumerate(pairs):
                for half in range(2):
                    hrows = slice(half * blk, (half + 1) * blk)
                    lse_tile = jnp.where(lane == 2 * pr + half, lse[i][hrows], lse_tile)
                o_ref[rows, cols[i]] = jnp.where(low_half, outs[i][0:blk], outs[i][blk:2 * blk]).astype(BF16)
        lse_ref[rows, :] = lse_tile


def _attention_group(arr4, cq, ck, cv, tq):
    b, dil, n_sub, _ = arr4.shape
    tq = min(tq, n_sub)
    n_blk = tq // ATTN_BLOCK

    def cur(c0):
        return pl.BlockSpec((None, None, tq, GROUP_WIDTH), lambda bi, r, n: (bi, r, n, c0))

    def prev(c0):
        return pl.BlockSpec((None, None, ATTN_BLOCK, GROUP_WIDTH),
                            lambda bi, r, n: (bi, r, jnp.maximum(n * n_blk - 1, 0), c0))

    return pl.pallas_call(
        functools.partial(_attn_kernel, n_blk=n_blk),
        grid=(b, dil, n_sub // tq),
        in_specs=[cur(cq), cur(ck), prev(ck), cur(cv), prev(cv)],
        out_specs=[
            pl.BlockSpec((None, None, tq, GROUP_WIDTH), lambda bi, r, n: (bi, r, n, 0)),
            pl.BlockSpec((None, None, tq, LANES), lambda bi, r, n: (bi, r, n, 0)),
        ],
        out_shape=[
            jax.ShapeDtypeStruct((b, dil, n_sub, GROUP_WIDTH), BF16),
            jax.ShapeDtypeStruct((b, dil, n_sub, LANES), F32),
        ],
        compiler_params=_params(("parallel", "parallel", "parallel")),
        name=f"attn_d{dil}",
    )(arr4, arr4, arr4, arr4, arr4)


def _deltanet_kernel(xb_ref, z_ref, ab_ref, convw_ref, alog_ref, dtb_ref, ognorm_ref, lcum_ref,
                     o_ref, qkv_scr, gc_scr, beta_scr, carry_scr, state_scr, ext_scr,
                     u_scr, w_scr, qd_scr, kt_scr, attn_scr, *, tile, chunks_per_iter=4):
    t_idx = pl.program_id(1)
    n_chunks = tile // CHUNK
    halo = 8
    heads = range(DN_HEADS)

    @pl.when(t_idx == 0)
    def _():
        carry_scr[...] = jnp.zeros_like(carry_scr)
        state_scr[...] = jnp.zeros_like(state_scr)

    for c in range(3 * DN_HEADS):
        cols = slice(c * DN_DIM, (c + 1) * DN_DIM)
        cur = xb_ref[:, cols].astype(F32)
        ext_scr[c, 0:halo, :] = carry_scr[:, cols]
        ext_scr[c, halo:halo + tile, :] = cur
        y = cur * convw_ref[CONV_WIDTH - 1:CONV_WIDTH, cols]
        for j in range(CONV_WIDTH - 1):
            off = halo - (CONV_WIDTH - 1) + j
            y = y + ext_scr[c, pl.ds(off, tile), :] * convw_ref[j:j + 1, cols]
        carry_scr[:, cols] = cur[tile - halo:tile]
        y = _silu(y)
        if c < 2 * DN_HEADS:
            y = y * lax.rsqrt(jnp.sum(y * y, axis=-1, keepdims=True) + EPS)
            if c < DN_HEADS:
                y = y * (DN_DIM ** -0.5)
        qkv_scr[:, cols] = y

    ab = ab_ref[...]
    beta_scr[...] = _sigmoid(ab)
    sp_in = ab + dtb_ref[...]
    softplus = jnp.maximum(sp_in, 0.0) + jnp.log(1.0 + jnp.exp(-jnp.abs(sp_in)))
    gdec = -jnp.exp(alog_ref[...]) * softplus
    g1, g2, g3 = _split3(gdec)
    lcum = lcum_ref[...]
    gc_scr[...] = _dot(lcum, g1) + _dot(lcum, g2) + _dot(lcum, g3)

    ri = lax.broadcasted_iota(jnp.int32, (CHUNK, CHUNK), 0)
    ci = lax.broadcasted_iota(jnp.int32, (CHUNK, CHUNK), 1)
    tril = ci <= ri
    tril_strict = ci < ri
    eye = (ci == ri).astype(F32)
    neg_inf = jnp.float32(-jnp.inf)

    def intra_body(c, carry):
        rows_c = [pl.ds(pl.multiple_of((c * chunks_per_iter + i) * CHUNK, CHUNK), CHUNK)
                  for i in range(chunks_per_iter)]
        units = [(i, h) for i in range(chunks_per_iter) for h in heads]
        gc_c = [gc_scr[r, :] for r in rows_c]
        gc_t_c = [jnp.transpose(g) for g in gc_c]
        beta_c = [beta_scr[r, :] for r in rows_c]
        q = [qkv_scr[rows_c[i], h * DN_DIM:(h + 1) * DN_DIM] for i, h in units]
        k = [qkv_scr[rows_c[i], DN_WIDTH + h * DN_DIM:DN_WIDTH + (h + 1) * DN_DIM] for i, h in units]
        v = [qkv_scr[rows_c[i], 2 * DN_WIDTH + h * DN_DIM:2 * DN_WIDTH + (h + 1) * DN_DIM] for i, h in units]

        b_col = [beta_c[i][:, h:h + 1] for i, h in units]
        g_col = [gc_c[i][:, DN_HEADS + h:DN_HEADS + h + 1] for i, h in units]
        g_row = [gc_t_c[i][DN_HEADS + h:DN_HEADS + h + 1, :] for i, h in units]
        g_last = [gcol[CHUNK - 1:CHUNK, :] for gcol in g_col]
        decay = [jnp.exp(jnp.where(tril, gcol - grow, neg_inf)) for gcol, grow in zip(g_col, g_row)]
        e_g = [jnp.exp(gcol) for gcol in g_col]
        kb = [kh * bh for kh, bh in zip(k, b_col)]
        k16 = [kh.astype(BF16) for kh in k]
        a_mat = [jnp.where(tril_strict, _dot_nt(kbh.astype(BF16), kh16) * dh, 0.0)
                 for kbh, kh16, dh in zip(kb, k16, decay)]
        pw = [-a for a in a_mat]
        t_mat = [eye + p for p in pw]
        for _ in range(5):
            pw16 = [p.astype(BF16) for p in pw]
            pw = [_dot(p, p) for p in pw16]
            t_mat = [th + _dot(th.astype(BF16), p.astype(BF16)) for th, p in zip(t_mat, pw)]
        t16 = [th.astype(BF16) for th in t_mat]
        u = [_dot(th, (vh * bh).astype(BF16)) for th, vh, bh in zip(t16, v, b_col)]
        w = [_dot(th, (kbh * eh).astype(BF16)) for th, kbh, eh in zip(t16, kb, e_g)]
        attn = [jnp.where(tril, _dot_nt(qh.astype(BF16), kh16) * dh, 0.0).astype(BF16)
                for qh, kh16, dh in zip(q, k16, decay)]
        q_dec = [(qh * eh).astype(BF16) for qh, eh in zip(q, e_g)]
        k_tail = [(kh * jnp.exp(gl - gcol)).astype(BF16) for kh, gl, gcol in zip(k, g_last, g_col)]
        for n, (i, h) in enumerate(units):
            hc = slice(h * DN_DIM, (h + 1) * DN_DIM)
            u_scr[rows_c[i], hc] = u[n]
            w_scr[rows_c[i], hc] = w[n].astype(BF16)
            qd_scr[rows_c[i], hc] = q_dec[n]
            kt_scr[rows_c[i], hc] = k_tail[n]
            attn_scr[h, rows_c[i], :] = attn[n]
        return carry

    lax.fori_loop(0, n_chunks // chunks_per_iter, intra_body, 0)

    def scan_body(c, carry):
        r0 = pl.multiple_of(c * CHUNK, CHUNK)
        rows = pl.ds(r0, CHUNK)
        decay_last = jnp.exp(gc_scr[pl.ds(r0 + CHUNK - 1, 1), :])
        hcs = [slice(h * DN_DIM, (h + 1) * DN_DIM) for h in heads]
        state = [state_scr[h] for h in heads]
        u = [u_scr[rows, hc] for hc in hcs]
        w16 = [w_scr[rows, hc] for hc in hcs]
        qd = [qd_scr[rows, hc] for hc in hcs]
        kt = [kt_scr[rows, hc] for hc in hcs]
        attn = [attn_scr[h, rows, :] for h in heads]
        z = [z_ref[rows, hc] for hc in hcs]
        s16 = [sh.astype(BF16) for sh in state]
        v_new = [(uh - _dot(wh, sh)).astype(BF16) for uh, wh, sh in zip(u, w16, s16)]
        o = [_dot(qh, sh) + _dot(ah, vn) for qh, sh, ah, vn in zip(qd, s16, attn, v_new)]
        new_state = [sh * decay_last[:, DN_HEADS + h:DN_HEADS + h + 1] + _dot_tn(kth, vn)
                     for h, sh, kth, vn in zip(heads, state, kt, v_new)]
        o = [oh * lax.rsqrt(jnp.mean(oh * oh, axis=-1, keepdims=True) + EPS) * ognorm_ref[...] for oh in o]
        o = [(oh * _silu(zh.astype(F32))).astype(BF16) for oh, zh in zip(o, z)]
        for h in heads:
            state_scr[h] = new_state[h]
        for h in heads:
            o_ref[rows, hcs[h]] = o[h]
        return carry

    lax.fori_loop(0, n_chunks, scan_body, 0)


def _deltanet(proj3, ab3, conv_w, a_log_l, dtb_l, og_norm, tile):
    b, s, w = proj3.shape
    lcum = jnp.tril(jnp.ones((tile, tile), F32))
    same_chunk = (jnp.arange(tile)[:, None] // CHUNK) == (jnp.arange(tile)[None, :] // CHUNK)
    lcum = jnp.where(same_chunk, lcum, 0.0).astype(BF16)
    return pl.pallas_call(
        functools.partial(_deltanet_kernel, tile=tile),
        grid=(b, s // tile),
        in_specs=[
            pl.BlockSpec((None, tile, 3 * DN_WIDTH), lambda bi, t: (bi, t, COL_B // (3 * DN_WIDTH))),
            pl.BlockSpec((None, tile, DN_WIDTH), lambda bi, t: (bi, t, COL_Z // DN_WIDTH)),
            pl.BlockSpec((None, tile, LANES), lambda bi, t: (bi, t, 0)),
            pl.BlockSpec((CONV_WIDTH, 3 * DN_WIDTH), lambda bi, t: (0, 0)),
            pl.BlockSpec((1, LANES), lambda bi, t: (0, 0)),
            pl.BlockSpec((1, LANES), lambda bi, t: (0, 0)),
            pl.BlockSpec((1, DN_DIM), lambda bi, t: (0, 0)),
            pl.BlockSpec((tile, tile), lambda bi, t: (0, 0)),
        ],
        out_specs=pl.BlockSpec((None, tile, DN_WIDTH), lambda bi, t: (bi, t, 0)),
        out_shape=jax.ShapeDtypeStruct((b, s, DN_WIDTH), BF16),
        scratch_shapes=[
            pltpu.VMEM((tile, 3 * DN_WIDTH), F32),
            pltpu.VMEM((tile, LANES), F32),
            pltpu.VMEM((tile, LANES), F32),
            pltpu.VMEM((8, 3 * DN_WIDTH), F32),
            pltpu.VMEM((DN_HEADS, DN_DIM, DN_DIM), F32),
            pltpu.VMEM((3 * DN_HEADS, 8 + tile, DN_DIM), F32),
            pltpu.VMEM((tile, DN_WIDTH), F32),
            pltpu.VMEM((tile, DN_WIDTH), BF16),
            pltpu.VMEM((tile, DN_WIDTH), BF16),
            pltpu.VMEM((tile, DN_WIDTH), BF16),
            pltpu.VMEM((DN_HEADS, tile, CHUNK), BF16),
        ],
        compiler_params=_params(("parallel", "arbitrary")),
        name="deltanet",
    )(proj3, proj3, ab3, conv_w, a_log_l, dtb_l, og_norm, lcum)


def _merge_kernel(o0_ref, o1_ref, o2_ref, l0_ref, l1_ref, l2_ref, ob_ref, ga_ref, gb_ref, x_ref,
                  wa_ref, wb_ref, wo_ref, hexp_ref, gain_ref, wr_hi_ref, wr_lo_ref, rbias_ref,
                  x1_ref, h2_ref, logit_ref, o1_scr, o2_scr, l1_scr, l2_scr, *, tm):
    for g, src, dst in ((1, o1_ref, o1_scr), (2, o2_ref, o2_scr), (1, l1_ref, l1_scr), (2, l2_ref, l2_scr)):
        dil = ATTN_GROUPS[g][1]
        for r in range(dil):
            for c in range(dst.shape[0]):
                dst[c, pl.ds(r, tm // dil, stride=dil), :] = src[r, :, c * LANES:(c + 1) * LANES].astype(F32)

    def slabs(scr):
        return jnp.concatenate([scr[c] for c in range(scr.shape[0])], axis=-1)

    l0, l1, l2 = l0_ref[...], l1_scr[0], l2_scr[0]
    m = jnp.maximum(jnp.maximum(l0, l1), l2)
    e0, e1, e2 = jnp.exp(l0 - m), jnp.exp(l1 - m), jnp.exp(l2 - m)
    tot = e0 + e1 + e2
    hexp = hexp_ref[...]
    o_a = jnp.zeros(o0_ref.shape, F32)
    for e, o_val in ((e0, o0_ref[...].astype(F32)), (e1, slabs(o1_scr)), (e2, slabs(o2_scr))):
        w_hi, w_lo = _split2(e / tot)
        o_a = o_a + (_dot(w_hi, hexp) + _dot(w_lo, hexp)) * o_val
    ya = _dot(o_a.astype(BF16), wa_ref[...])
    yb = _dot(ob_ref[...], wb_ref[...])
    merged = _sigmoid(ga_ref[...].astype(F32)) * ya + _sigmoid(gb_ref[...].astype(F32)) * yb
    x1 = x_ref[...] + _dot(merged.astype(BF16), wo_ref[...])
    x1_ref[...] = x1
    h2 = x1 * lax.rsqrt(jnp.mean(x1 * x1, axis=-1, keepdims=True) + EPS) * gain_ref[...]
    hi, lo = _split2(h2)
    h2_ref[...] = hi
    logit_ref[...] = (_dot(hi, wr_hi_ref[...]) + _dot(lo, wr_hi_ref[...]) + _dot(hi, wr_lo_ref[...])
                      + rbias_ref[...])


def _merge(o0, l0, o_perm, l_perm, o_b, proj, x2, wa, wb, wo, gain, wr_hi, wr_lo, rbias, s, tm):
    t, d = x2.shape
    tps = s // tm
    head_of_col = jnp.arange(GROUP_WIDTH) // HEAD_DIM_A
    hexp = (jnp.arange(LANES)[:, None] == head_of_col[None, :]).astype(BF16)

    def row(wd, cblk=0):
        return pl.BlockSpec((tm, wd), lambda i: (i, cblk))

    def perm(arr):
        _, dil, _, wd = arr.shape
        return pl.BlockSpec((None, dil, tm // dil, wd), lambda i: (i // tps, 0, i % tps, 0))

    def full(a):
        return pl.BlockSpec(a.shape, lambda i: (0, 0))

    return pl.pallas_call(
        functools.partial(_merge_kernel, tm=tm),
        grid=(t // tm,),
        in_specs=[row(GROUP_WIDTH), perm(o_perm[0]), perm(o_perm[1]),
                  row(LANES), perm(l_perm[0]), perm(l_perm[1]),
                  row(DN_WIDTH), row(d, COL_GA // d), row(d, COL_GB // d), row(d),
                  full(wa), full(wb), full(wo), full(hexp), full(gain), full(wr_hi), full(wr_lo),
                  full(rbias)],
        out_specs=[row(d), row(d), row(LANES)],
        out_shape=[
            jax.ShapeDtypeStruct((t, d), F32),
            jax.ShapeDtypeStruct((t, d), BF16),
            jax.ShapeDtypeStruct((t, LANES), F32),
        ],
        scratch_shapes=[pltpu.VMEM((GROUP_WIDTH // LANES, tm, LANES), F32),
                        pltpu.VMEM((GROUP_WIDTH // LANES, tm, LANES), F32),
                        pltpu.VMEM((1, tm, LANES), F32), pltpu.VMEM((1, tm, LANES), F32)],
        compiler_params=_params(("parallel",)),
        name="merge",
    )(o0, o_perm[0], o_perm[1], l0, l_perm[0], l_perm[1], o_b, proj, proj, x2,
      wa, wb, wo, hexp, gain, wr_hi, wr_lo, rbias)


def _routing_weights(logits):
    lane_i = lax.broadcasted_iota(jnp.int32, logits.shape, 1)
    lane = lane_i.astype(F32)
    neg_inf = jnp.float32(-jnp.inf)
    big = jnp.float32(1 << 20)
    is_group = lane_i < N_EXPERT_GROUPS
    gl = jnp.where(is_group, logits, neg_inf)
    gmax = jnp.max(gl, axis=-1, keepdims=True)
    gsum = jnp.sum(jnp.exp(gl - gmax), axis=-1, keepdims=True)
    p_g = 1.0 / gsum
    g_idx = jnp.min(jnp.where(gl == gmax, lane, big), axis=-1, keepdims=True)
    n_exp = N_EXPERT_GROUPS * EXPERTS_PER_GROUP
    is_exp = jnp.logical_and(lane_i >= N_EXPERT_GROUPS, lane_i < N_EXPERT_GROUPS + n_exp)
    exp_group = jnp.right_shift(lane_i - N_EXPERT_GROUPS, 2).astype(F32)
    assert EXPERTS_PER_GROUP == 4
    sel = jnp.logical_and(is_exp, exp_group == g_idx)
    el = jnp.where(sel, logits, neg_inf)
    emax = jnp.max(el, axis=-1, keepdims=True)
    ee = jnp.exp(el - emax)
    esum = jnp.sum(ee, axis=-1, keepdims=True)
    idx1 = jnp.min(jnp.where(el == emax, lane, big), axis=-1, keepdims=True)
    el2 = jnp.where(lane == idx1, neg_inf, el)
    e2max = jnp.max(el2, axis=-1, keepdims=True)
    idx2 = jnp.min(jnp.where(el2 == e2max, lane, big), axis=-1, keepdims=True)
    p1 = 1.0 / esum
    p2 = jnp.exp(e2max - emax) / esum
    top_sum = p1 + p2
    w = jnp.where(lane == idx1, p1 / top_sum, jnp.where(lane == idx2, p2 / top_sum, 0.0))
    return w * p_g, g_idx


def _moe_kernel(h2_ref, logit_ref, x1_ref, wg_ref, wu_ref, wd_ref, lstrict_ref, out_ref,
                hs_scr, cws_scr, pos_scr, acc_scr, off_smem, *, tm, sub):
    g = pl.program_id(1)
    lane = lax.broadcasted_iota(jnp.int32, (tm, LANES), 1)

    @pl.when(g == 0)
    def _():
        cw, g_idx = _routing_weights(logit_ref[...])
        onehot = jnp.where(lane.astype(F32) == g_idx, 1.0, 0.0)
        earlier = _dot(lstrict_ref[...], onehot.astype(BF16))
        counts = jnp.sum(onehot, axis=0, keepdims=True)
        lane_row = lax.broadcasted_iota(jnp.int32, (1, LANES), 1)
        start = jnp.zeros((1, LANES), F32)
        for k in range(N_EXPERT_GROUPS - 1):
            start = start + jnp.where(lane_row > k, counts[:, k:k + 1], 0.0)
        pos = jnp.sum(onehot * (earlier + start), axis=-1, keepdims=True)
        pos_b = jnp.broadcast_to(pos, (tm, LANES))
        pos_scr[...] = pos_b
        pos_row = jnp.transpose(pos_b)[0:1, :]
        dest = lax.broadcasted_iota(jnp.int32, (tm, tm), 0).astype(F32)
        gather = jnp.where(dest == pos_row, 1.0, 0.0).astype(BF16)
        hs_scr[...] = _dot(gather, h2_ref[...]).astype(BF16)
        cw_hi, cw_lo = _split2(cw)
        cws_scr[...] = _dot(gather, cw_hi) + _dot(gather, cw_lo)
        for k in range(N_EXPERT_GROUPS):
            off_smem[k] = start[0, k].astype(jnp.int32)
        off_smem[N_EXPERT_GROUPS] = jnp.int32(tm)
        acc_scr[...] = jnp.zeros_like(acc_scr)

    first = off_smem[g]
    last = off_smem[g + 1]
    lane_s = lax.broadcasted_iota(jnp.int32, (sub, LANES), 1)
    for s in range(tm // sub):
        @pl.when(jnp.logical_and(first < (s + 1) * sub, last > s * sub))
        def _(s=s):
            rows = slice(s * sub, (s + 1) * sub)
            hs = hs_scr[rows, :]
            hg = _dot(hs, wg_ref[...])
            hu = _dot(hs, wu_ref[...])
            cw = cws_scr[rows, :]
            parts = []
            for j in range(EXPERTS_PER_GROUP):
                cols = slice(j * D_EXPERT, (j + 1) * D_EXPERT)
                want = N_EXPERT_GROUPS + g * EXPERTS_PER_GROUP + j
                c_col = jnp.sum(jnp.where(lane_s == want, cw, 0.0), axis=-1, keepdims=True)
                parts.append((_silu(hg[:, cols]) * hu[:, cols] * c_col).astype(BF16))
            act = jnp.concatenate(parts, axis=-1)
            acc_scr[rows, :] += _dot(act, wd_ref[...])

    @pl.when(g == N_EXPERT_GROUPS - 1)
    def _():
        src = lax.broadcasted_iota(jnp.int32, (tm, tm), 1).astype(F32)
        scatter = jnp.where(pos_scr[:, 0:1] == src, 1.0, 0.0).astype(BF16)
        out_ref[...] = x1_ref[...] + _dot(scatter, acc_scr[...].astype(BF16))


def _moe(h2, logits, x1, wg, wu, wd, tm, sub):
    t, d = x1.shape
    gw = EXPERTS_PER_GROUP * D_EXPERT
    lstrict = jnp.tril(jnp.ones((tm, tm), F32), -1).astype(BF16)
    return pl.pallas_call(
        functools.partial(_moe_kernel, tm=tm, sub=sub),
        grid=(t // tm, N_EXPERT_GROUPS),
        in_specs=[
            pl.BlockSpec((tm, d), lambda i, g: (i, 0)),
            pl.BlockSpec((tm, LANES), lambda i, g: (i, 0)),
            pl.BlockSpec((tm, d), lambda i, g: (i, 0)),
            pl.BlockSpec((None, d, gw), lambda i, g: (g, 0, 0)),
            pl.BlockSpec((None, d, gw), lambda i, g: (g, 0, 0)),
            pl.BlockSpec((None, gw, d), lambda i, g: (g, 0, 0)),
            pl.BlockSpec((tm, tm), lambda i, g: (0, 0)),
        ],
        out_specs=pl.BlockSpec((tm, d), lambda i, g: (i, 0)),
        out_shape=jax.ShapeDtypeStruct((t, d), F32),
        scratch_shapes=[pltpu.VMEM((tm, d), BF16), pltpu.VMEM((tm, LANES), F32),
                        pltpu.VMEM((tm, LANES), F32), pltpu.VMEM((tm, d), F32),
                        pltpu.SMEM((8,), jnp.int32)],
        compiler_params=_params(("parallel", "arbitrary")),
        name="moe",
    )(h2, logits, x1, wg, wu, wd, lstrict)


def _ple_kernel(x_ref, p_ref, gain_ref, wpg_ref, wple_ref, out_ref):
    x = x_ref[...]
    h3 = x * lax.rsqrt(jnp.mean(x * x, axis=-1, keepdims=True) + EPS) * gain_ref[...]
    gate = _sigmoid(_dot(h3.astype(BF16), wpg_ref[...]))
    ple = _dot(p_ref[...].astype(BF16), wple_ref[...])
    out_ref[...] = x + gate * ple


def _ple(x2, p2, gain, wpg, wple, tm):
    t, d = x2.shape
    pd = p2.shape[1]
    return pl.pallas_call(
        _ple_kernel,
        grid=(t // tm,),
        in_specs=[
            pl.BlockSpec((tm, d), lambda i: (i, 0)),
            pl.BlockSpec((tm, pd), lambda i: (i, 0)),
            pl.BlockSpec((1, d), lambda i: (0, 0)),
            pl.BlockSpec((d, d), lambda i: (0, 0)),
            pl.BlockSpec((pd, d), lambda i: (0, 0)),
        ],
        out_specs=pl.BlockSpec((tm, d), lambda i: (i, 0)),
        out_shape=jax.ShapeDtypeStruct((t, d), F32),
        compiler_params=_params(("parallel",)),
        name="ple",
    )(x2, p2, gain, wpg, wple)


def _lane_pad(v, offset):
    out = jnp.zeros((1, LANES), F32)
    return out.at[0, offset:offset + v.shape[0]].set(v.astype(F32))


def _layer(x2, p2, b, s, norm_mix, w_in, q_norm, k_norm, conv_w, a_log, dt_bias, dn_out_norm,
           w_branch_a, w_branch_b, w_out, norm_ffn, w_router_group, b_router_group,
           w_router_expert, b_router_expert, w_expert_gate, w_expert_up, w_expert_down,
           norm_ple, w_ple, w_ple_gate):
    t, d = x2.shape
    qa = N_GROUPS * GROUP_WIDTH
    o_b0 = 3 * qa
    o_z = o_b0 + 3 * DN_WIDTH
    o_beta = o_z + DN_WIDTH
    o_alpha = o_beta + DN_HEADS
    o_ga = o_alpha + DN_HEADS
    o_gb = o_ga + d
    w_in = w_in.astype(F32)
    gw_a = GROUP_WIDTH
    group_tiles = [jnp.concatenate([w_in[:, part * qa + g * gw_a:part * qa + (g + 1) * gw_a]
                                    for part in range(3)], axis=1) for g in range(N_GROUPS)]
    w_main = jnp.concatenate([
        w_in[:, o_b0:o_z],
        w_in[:, o_z:o_beta],
        w_in[:, o_ga:o_gb],
        w_in[:, o_gb:o_gb + d],
    ] + group_tiles, axis=1).astype(BF16)
    w_ab = jnp.zeros((d, LANES), F32).at[:, 0:2 * DN_HEADS].set(w_in[:, o_beta:o_ga])
    wab_hi, wab_lo = _split2(w_ab)
    qg = jnp.tile(q_norm.astype(F32), HEADS_A) * (HEAD_DIM_A ** -0.5 * LOG2E)
    kg = jnp.tile(k_norm.astype(F32), HEADS_A)
    qkgain = jnp.concatenate([qg, kg])[None, :]

    proj, ab, qkv_p1, qkv_p2 = _in_projection(x2, norm_mix.astype(F32)[None, :], w_main, wab_hi, wab_lo,
                                              qkgain, b, s, tm=min(1024, s))
    proj3 = proj.reshape(b, s, PROJ_WIDTH)

    cb = GROUP_WIDTH
    o0, l0 = _attention_group(proj3.reshape(b, 1, s, PROJ_WIDTH), COL_Q // cb, COL_K // cb, COL_V // cb, tq=512)
    o1, l1 = _attention_group(qkv_p1, 0, 1, 2, tq=512)
    o2, l2 = _attention_group(qkv_p2, 0, 1, 2, tq=512)

    o_b = _deltanet(proj3, ab.reshape(b, s, LANES), conv_w.astype(F32),
                    _lane_pad(a_log, DN_HEADS), _lane_pad(dt_bias, DN_HEADS),
                    dn_out_norm.astype(F32)[None, :], tile=min(512, s)).reshape(t, DN_WIDTH)

    w_r = jnp.zeros((d, LANES), F32)
    w_r = w_r.at[:, 0:N_EXPERT_GROUPS].set(w_router_group.astype(F32))
    n_exp = N_EXPERT_GROUPS * EXPERTS_PER_GROUP
    w_r = w_r.at[:, N_EXPERT_GROUPS:N_EXPERT_GROUPS + n_exp].set(w_router_expert.astype(F32))
    wr_hi, wr_lo = _split2(w_r)
    rbias = _lane_pad(jnp.concatenate([b_router_group.astype(F32), b_router_expert.astype(F32)]), 0)

    x1, h2, logits = _merge(o0.reshape(t, GROUP_WIDTH), l0.reshape(t, LANES), (o1, o2), (l1, l2), o_b,
                            proj, x2, w_branch_a.astype(BF16), w_branch_b.astype(BF16),
                            w_out.astype(BF16), norm_ffn.astype(F32)[None, :], wr_hi, wr_lo, rbias,
                            s, tm=min(512, s))

    gw = EXPERTS_PER_GROUP * D_EXPERT
    wg = jnp.transpose(w_expert_gate, (0, 2, 1, 3)).reshape(N_EXPERT_GROUPS, d, gw).astype(BF16)
    wu = jnp.transpose(w_expert_up, (0, 2, 1, 3)).reshape(N_EXPERT_GROUPS, d, gw).astype(BF16)
    wd = w_expert_down.reshape(N_EXPERT_GROUPS, gw, d).astype(BF16)
    x2n = _moe(h2, logits, x1, wg, wu, wd, tm=min(1024, t), sub=256)

    return _ple(x2n, p2, norm_ple.astype(F32)[None, :], w_ple_gate.astype(BF16), w_ple.astype(BF16),
                tm=min(1024, t))


def kernel(x, p, norm_mix, w_in, q_norm, k_norm, conv_w, a_log, dt_bias, dn_out_norm, w_branch_a, w_branch_b, w_out, norm_ffn, w_router_group, b_router_group, w_router_expert, b_router_expert, w_expert_gate, w_expert_up, w_expert_down, norm_ple, w_ple, w_ple_gate):
    b, s, d = x.shape
    depth = w_in.shape[0]
    x2 = x.astype(F32).reshape(b * s, d)
    for i in range(depth):
        x2 = _layer(x2, p[i].reshape(b * s, -1), b, s, norm_mix[i], w_in[i], q_norm[i], k_norm[i],
                    conv_w[i], a_log[i], dt_bias[i], dn_out_norm[i], w_branch_a[i], w_branch_b[i],
                    w_out[i], norm_ffn[i], w_router_group[i], b_router_group[i], w_router_expert[i],
                    b_router_expert[i], w_expert_gate[i], w_expert_up[i], w_expert_down[i],
                    norm_ple[i], w_ple[i], w_ple_gate[i])
    return x2.reshape(b, s, d)
```

```python
import functools

import jax
import jax.numpy as jnp
from jax import lax
from jax.experimental import pallas as pl
from jax.experimental.pallas import tpu as pltpu

F32 = jnp.float32
BF16 = jnp.bfloat16
EPS = 1e-6
LN2 = 0.6931471805599453
LOG2E = 1.4426950408889634

ATTN_GROUPS = ((128, 1), (512, 4), (2048, 16))
N_GROUPS = 3
HEADS_A = 8
HEAD_DIM_A = 64
GROUP_WIDTH = HEADS_A * HEAD_DIM_A
ATTN_BLOCK = 128
DN_HEADS = 8
DN_DIM = 128
DN_WIDTH = DN_HEADS * DN_DIM
CONV_WIDTH = 4
CHUNK = 64
N_EXPERT_GROUPS = 4
EXPERTS_PER_GROUP = 4
D_EXPERT = 256
SUBLANES = 8
LANES = 128

PROJ_TILE = 1536
COL_B, COL_Z, COL_GA, COL_GB = 0, 3072, 4096, 5120
PROJ_WIDTH = 6144

V7X_VMEM_BYTES = 64 * 1024 * 1024
VMEM_LIMIT = V7X_VMEM_BYTES * 7 // 8


def _tiles(s):
    return dict(
        in_proj=min(1024, s),
        attn=512,
        deltanet=min(512, s),
        merge=min(512, s),
        moe=min(1024, s),
        moe_slab=256,
        ple=min(1024, s),
    )


def _dot(a, b):
    return jnp.dot(a, b, preferred_element_type=F32)


def _dot_nt(a, b):
    return lax.dot_general(a, b, (((1,), (1,)), ((), ())), preferred_element_type=F32)


def _dot_tn(a, b):
    return lax.dot_general(a, b, (((0,), (0,)), ((), ())), preferred_element_type=F32)


def _split2(v):
    hi = v.astype(BF16)
    lo = (v - hi.astype(F32)).astype(BF16)
    return hi, lo


def _split3(v):
    a = v.astype(BF16)
    r = v - a.astype(F32)
    b = r.astype(BF16)
    c = (r - b.astype(F32)).astype(BF16)
    return a, b, c


def _sigmoid(v):
    return 0.5 + 0.5 * jnp.tanh(0.5 * v)


def _silu(v):
    half = 0.5 * v
    return half + half * jnp.tanh(half)


def _params(sem):
    return pltpu.CompilerParams(dimension_semantics=sem, vmem_limit_bytes=VMEM_LIMIT)


def _inproj_main_kernel(x_ref, gain_ref, w_ref, wab_ref, proj_ref, ab_ref, h_ref):
    @pl.when(pl.program_id(1) == 0)
    def _():
        x = x_ref[...]
        ms = jnp.mean(x * x, axis=-1, keepdims=True)
        h = x * lax.rsqrt(ms + EPS) * gain_ref[...]
        hi, lo = _split2(h)
        h_ref[...] = hi
        hi_terms = _dot(hi, wab_ref[...])
        ab_ref[...] = hi_terms[:, :LANES] + hi_terms[:, LANES:] + _dot(lo, wab_ref[:, :LANES])

    proj_ref[...] = _dot(h_ref[...], w_ref[...]).astype(BF16)


def _inproj_attn_kernel(h_ref, w_ref, qkgain_ref, red_ref, exp_ref, out_ref, perm_scr, *, tm, g):
    slabs = GROUP_WIDTH // LANES

    def normed(a, gain):
        ss = _dot((a * a).astype(BF16), red_ref[...])
        r = lax.rsqrt(ss * (1.0 / HEAD_DIM_A) + EPS)
        r_hi, r_lo = _split2(r)
        rexp = _dot(jnp.concatenate([r_hi, r_lo], axis=1), exp_ref[...])
        return a * rexp * gain

    h = h_ref[...]
    dil = ATTN_GROUPS[g][1]
    n_u = tm // dil
    for p in range(3):
        pcols = slice(p * GROUP_WIDTH, (p + 1) * GROUP_WIDTH)
        val = _dot(h, w_ref[:, pcols])
        if p < 2:
            val = normed(val, qkgain_ref[:, pcols])
        if dil == 1:
            out_ref[0, :, pcols] = val.astype(BF16)
        else:
            if dil < SUBLANES:
                for c in range(slabs):
                    perm_scr[c, 0:tm, :] = val[:, c * LANES:(c + 1) * LANES]
                for r in range(dil):
                    for c in range(slabs):
                        lo = p * GROUP_WIDTH + c * LANES
                        out_ref[r, :, lo:lo + LANES] = (
                            perm_scr[c, pl.ds(r, n_u, stride=dil), :].astype(BF16))
            else:
                pitch = n_u + 1
                for c in range(slabs):
                    for m in range(tm // SUBLANES):
                        u, r0 = divmod(m * SUBLANES, dil)
                        perm_scr[c, pl.ds(r0 * pitch + u, SUBLANES, stride=pitch), :] = (
                            val[m * SUBLANES:(m + 1) * SUBLANES, c * LANES:(c + 1) * LANES])
                for r in range(dil):
                    for c in range(slabs):
                        lo = p * GROUP_WIDTH + c * LANES
                        out_ref[r, :, lo:lo + LANES] = perm_scr[c, pl.ds(r * pitch, n_u), :].astype(BF16)


def _in_projection(x2, norm_gain, w_plain, w_groups, wab, qkgain, b, s, tm):
    t, d = x2.shape
    assert w_plain.shape[1] == PROJ_WIDTH and PROJ_WIDTH % PROJ_TILE == 0
    proj, ab, h = pl.pallas_call(
        _inproj_main_kernel,
        grid=(t // tm, PROJ_WIDTH // PROJ_TILE),
        in_specs=[
            pl.BlockSpec((tm, d), lambda i, j: (i, 0)),
            pl.BlockSpec((1, d), lambda i, j: (0, 0)),
            pl.BlockSpec((d, PROJ_TILE), lambda i, j: (0, j)),
            pl.BlockSpec((d, 2 * LANES), lambda i, j: (0, 0)),
        ],
        out_specs=[
            pl.BlockSpec((tm, PROJ_TILE), lambda i, j: (i, j)),
            pl.BlockSpec((tm, LANES), lambda i, j: (i, 0)),
            pl.BlockSpec((tm, d), lambda i, j: (i, 0)),
        ],
        out_shape=[
            jax.ShapeDtypeStruct((t, PROJ_WIDTH), BF16),
            jax.ShapeDtypeStruct((t, LANES), F32),
            jax.ShapeDtypeStruct((t, d), BF16),
        ],
        compiler_params=_params(("parallel", "arbitrary")),
        name="in_proj",
    )(x2, norm_gain, w_plain, wab)

    tps = s // tm
    col = jnp.arange(GROUP_WIDTH) // HEAD_DIM_A
    red = (col[:, None] == jnp.arange(LANES)[None, :]).astype(BF16)
    expm2 = jnp.concatenate([red.T, red.T], axis=0)
    qkv = []
    for g, (_, dil) in enumerate(ATTN_GROUPS):
        qkv.append(pl.pallas_call(
            functools.partial(_inproj_attn_kernel, tm=tm, g=g),
            grid=(t // tm,),
            in_specs=[
                pl.BlockSpec((tm, d), lambda i: (i, 0)),
                pl.BlockSpec((d, PROJ_TILE), lambda i: (0, 0)),
                pl.BlockSpec((1, 2 * GROUP_WIDTH), lambda i: (0, 0)),
                pl.BlockSpec((GROUP_WIDTH, LANES), lambda i: (0, 0)),
                pl.BlockSpec((2 * LANES, GROUP_WIDTH), lambda i: (0, 0)),
            ],
            out_specs=pl.BlockSpec((None, dil, tm // dil, PROJ_TILE), lambda i: (i // tps, 0, i % tps, 0)),
            out_shape=jax.ShapeDtypeStruct((b, dil, s // dil, PROJ_TILE), BF16),
            scratch_shapes=[pltpu.VMEM((GROUP_WIDTH // LANES, tm + dil, LANES), F32)],
            compiler_params=_params(("parallel",)),
            name=f"in_proj_attn_d{dil}",
        )(h, w_groups[g], qkgain, red, expm2))
    return proj, ab, qkv


def _attn_kernel(q_ref, kc_ref, kp_ref, vc_ref, vp_ref, o_ref, lse_ref, *, n_blk, pairs_at_once=4):
    n = pl.program_id(2)
    blk = ATTN_BLOCK
    qi = lax.broadcasted_iota(jnp.int32, (2 * blk, blk), 0) & (blk - 1)
    kj = lax.broadcasted_iota(jnp.int32, (2 * blk, blk), 1)
    cur_ok2 = kj <= qi
    prev_band = kj >= qi
    lane = lax.broadcasted_iota(jnp.int32, (blk, LANES), 1)
    low_half = lane < HEAD_DIM_A
    neg_inf = jnp.float32(-jnp.inf)
    ones = jnp.ones((blk, LANES), BF16)
    zero = jnp.zeros((blk, LANES), BF16)

    for b in range(n_blk):
        rows = slice(b * blk, (b + 1) * blk)
        if b == 0:
            prev_ok2 = jnp.logical_and(prev_band, n > 0)
        else:
            prev_ok2 = prev_band
        lse_tile = jnp.zeros((blk, LANES), F32)
        for p0 in range(0, HEADS_A // 2, pairs_at_once):
            pairs = range(p0, p0 + pairs_at_once)
            cols = [slice(pr * LANES, (pr + 1) * LANES) for pr in pairs]
            q2 = [q_ref[rows, c] for c in cols]
            kc2 = [kc_ref[rows, c] for c in cols]
            vc2 = [vc_ref[rows, c] for c in cols]
            if b == 0:
                kp2 = [kp_ref[:, c] for c in cols]
                vp2 = [vp_ref[:, c] for c in cols]
            else:
                prow = slice((b - 1) * blk, b * blk)
                kp2 = [kc_ref[prow, c] for c in cols]
                vp2 = [vc_ref[prow, c] for c in cols]
            q_st = [jnp.concatenate([jnp.where(low_half, q, zero), jnp.where(low_half, zero, q)], axis=0)
                    for q in q2]
            vc_aug = [jnp.concatenate([x, ones], axis=1) for x in vc2]
            vp_aug = [jnp.concatenate([x, ones], axis=1) for x in vp2]
            s_c = [jnp.where(cur_ok2, _dot_nt(q, kk), neg_inf) for q, kk in zip(q_st, kc2)]
            s_p = [jnp.where(prev_ok2, _dot_nt(q, kk), neg_inf) for q, kk in zip(q_st, kp2)]
            m = [jnp.maximum(jnp.max(sc, axis=-1, keepdims=True), jnp.max(sp, axis=-1, keepdims=True))
                 for sc, sp in zip(s_c, s_p)]
            e_c = [jnp.exp2(sc - mh).astype(BF16) for sc, mh in zip(s_c, m)]
            e_p = [jnp.exp2(sp - mh).astype(BF16) for sp, mh in zip(s_p, m)]
            nd = [_dot(ec, va) + _dot(ep, vb) for ec, va, ep, vb in zip(e_c, vc_aug, e_p, vp_aug)]
            outs = [x[:, :LANES] / x[:, LANES:] for x in nd]
            lse = [mh * LN2 + jnp.log(x[:, LANES:LANES + 1]) for mh, x in zip(m, nd)]
            for i, pr in enumerate(pairs):
                for half in range(2):
                    hrows = slice(half * blk, (half + 1) * blk)
                    lse_tile = jnp.where(lane == 2 * pr + half, lse[i][hrows], lse_tile)
                o_ref[rows, cols[i]] = jnp.where(low_half, outs[i][0:blk], outs[i][blk:2 * blk]).astype(BF16)
        lse_ref[rows, :] = lse_tile


def _attention_group(arr4, cq, ck, cv, tq):
    b, dil, n_sub, _ = arr4.shape
    tq = min(tq, n_sub)
    n_blk = tq // ATTN_BLOCK

    def cur(c0):
        return pl.BlockSpec((None, None, tq, GROUP_WIDTH), lambda bi, r, n: (bi, r, n, c0))

    def prev(c0):
        return pl.BlockSpec((None, None, ATTN_BLOCK, GROUP_WIDTH),
                            lambda bi, r, n: (bi, r, jnp.maximum(n * n_blk - 1, 0), c0))

    return pl.pallas_call(
        functools.partial(_attn_kernel, n_blk=n_blk),
        grid=(b, dil, n_sub // tq),
        in_specs=[cur(cq), cur(ck), prev(ck), cur(cv), prev(cv)],
        out_specs=[
            pl.BlockSpec((None, None, tq, GROUP_WIDTH), lambda bi, r, n: (bi, r, n, 0)),
            pl.BlockSpec((None, None, tq, LANES), lambda bi, r, n: (bi, r, n, 0)),
        ],
        out_shape=[
            jax.ShapeDtypeStruct((b, dil, n_sub, GROUP_WIDTH), BF16),
            jax.ShapeDtypeStruct((b, dil, n_sub, LANES), F32),
        ],
        compiler_params=_params(("parallel", "parallel", "parallel")),
        name=f"attn_d{dil}",
    )(arr4, arr4, arr4, arr4, arr4)


def _deltanet_kernel(xb_ref, z_ref, ab_ref, convw_ref, alog_ref, dtb_ref, ognorm_ref, lcum_ref,
                     o_ref, qkv_scr, gc_scr, beta_scr, carry_scr, state_scr, ext_scr,
                     u_scr, w_scr, qd_scr, kt_scr, attn_scr, *, tile, chunks_per_iter=4):
    t_idx = pl.program_id(1)
    n_chunks = tile // CHUNK
    halo = 8
    heads = range(DN_HEADS)

    @pl.when(t_idx == 0)
    def _():
        carry_scr[...] = jnp.zeros_like(carry_scr)
        state_scr[...] = jnp.zeros_like(state_scr)

    for c in range(3 * DN_HEADS):
        cols = slice(c * DN_DIM, (c + 1) * DN_DIM)
        cur = xb_ref[:, cols].astype(F32)
        ext_scr[c, 0:halo, :] = carry_scr[:, cols]
        ext_scr[c, halo:halo + tile, :] = cur
        y = cur * convw_ref[CONV_WIDTH - 1:CONV_WIDTH, cols]
        for j in range(CONV_WIDTH - 1):
            off = halo - (CONV_WIDTH - 1) + j
            y = y + ext_scr[c, pl.ds(off, tile), :] * convw_ref[j:j + 1, cols]
        carry_scr[:, cols] = cur[tile - halo:tile]
        y = _silu(y)
        if c < 2 * DN_HEADS:
            y = y * lax.rsqrt(jnp.sum(y * y, axis=-1, keepdims=True) + EPS)
            if c < DN_HEADS:
                y = y * (DN_DIM ** -0.5)
        qkv_scr[:, cols] = y

    ab = ab_ref[...]
    beta_scr[...] = _sigmoid(ab)
    sp_in = ab + dtb_ref[...]
    softplus = jnp.maximum(sp_in, 0.0) + jnp.log(1.0 + jnp.exp(-jnp.abs(sp_in)))
    gdec = -jnp.exp(alog_ref[...]) * softplus
    g1, g2, g3 = _split3(gdec)
    lcum = lcum_ref[...]
    gc_scr[...] = _dot(lcum, g1) + _dot(lcum, g2) + _dot(lcum, g3)

    ri = lax.broadcasted_iota(jnp.int32, (CHUNK, CHUNK), 0)
    ci = lax.broadcasted_iota(jnp.int32, (CHUNK, CHUNK), 1)
    tril = ci <= ri
    tril_strict = ci < ri
    eye = (ci == ri).astype(F32)
    neg_inf = jnp.float32(-jnp.inf)

    def intra_body(c, carry):
        rows_c = [pl.ds(pl.multiple_of((c * chunks_per_iter + i) * CHUNK, CHUNK), CHUNK)
                  for i in range(chunks_per_iter)]
        units = [(i, h) for i in range(chunks_per_iter) for h in heads]
        gc_c = [gc_scr[r, :] for r in rows_c]
        gc_t_c = [jnp.transpose(g) for g in gc_c]
        beta_c = [beta_scr[r, :] for r in rows_c]
        q = [qkv_scr[rows_c[i], h * DN_DIM:(h + 1) * DN_DIM] for i, h in units]
        k = [qkv_scr[rows_c[i], DN_WIDTH + h * DN_DIM:DN_WIDTH + (h + 1) * DN_DIM] for i, h in units]
        v = [qkv_scr[rows_c[i], 2 * DN_WIDTH + h * DN_DIM:2 * DN_WIDTH + (h + 1) * DN_DIM] for i, h in units]

        b_col = [beta_c[i][:, h:h + 1] for i, h in units]
        g_col = [gc_c[i][:, DN_HEADS + h:DN_HEADS + h + 1] for i, h in units]
        g_row = [gc_t_c[i][DN_HEADS + h:DN_HEADS + h + 1, :] for i, h in units]
        g_last = [gcol[CHUNK - 1:CHUNK, :] for gcol in g_col]
        decay = [jnp.exp(jnp.where(tril, gcol - grow, neg_inf)) for gcol, grow in zip(g_col, g_row)]
        e_g = [jnp.exp(gcol) for gcol in g_col]
        kb = [kh * bh for kh, bh in zip(k, b_col)]
        k16 = [kh.astype(BF16) for kh in k]
        a_mat = [jnp.where(tril_strict, _dot_nt(kbh.astype(BF16), kh16) * dh, 0.0)
                 for kbh, kh16, dh in zip(kb, k16, decay)]
        pw = [-a for a in a_mat]
        t_mat = [eye + p for p in pw]
        for _ in range(5):
            pw16 = [p.astype(BF16) for p in pw]
            pw = [_dot(p, p) for p in pw16]
            t_mat = [th + _dot(th.astype(BF16), p.astype(BF16)) for th, p in zip(t_mat, pw)]
        t16 = [th.astype(BF16) for th in t_mat]
        u = [_dot(th, (vh * bh).astype(BF16)) for th, vh, bh in zip(t16, v, b_col)]
        w = [_dot(th, (kbh * eh).astype(BF16)) for th, kbh, eh in zip(t16, kb, e_g)]
        attn = [jnp.where(tril, _dot_nt(qh.astype(BF16), kh16) * dh, 0.0).astype(BF16)
                for qh, kh16, dh in zip(q, k16, decay)]
        q_dec = [(qh * eh).astype(BF16) for qh, eh in zip(q, e_g)]
        k_tail = [(kh * jnp.exp(gl - gcol)).astype(BF16) for kh, gl, gcol in zip(k, g_last, g_col)]
        for n, (i, h) in enumerate(units):
            hc = slice(h * DN_DIM, (h + 1) * DN_DIM)
            u_scr[rows_c[i], hc] = u[n]
            w_scr[rows_c[i], hc] = w[n].astype(BF16)
            qd_scr[rows_c[i], hc] = q_dec[n]
            kt_scr[rows_c[i], hc] = k_tail[n]
            attn_scr[h, rows_c[i], :] = attn[n]
        return carry

    lax.fori_loop(0, n_chunks // chunks_per_iter, intra_body, 0)

    def scan_body(c, carry):
        r0 = pl.multiple_of(c * CHUNK, CHUNK)
        rows = pl.ds(r0, CHUNK)
        decay_last = jnp.exp(gc_scr[pl.ds(r0 + CHUNK - 1, 1), :])
        hcs = [slice(h * DN_DIM, (h + 1) * DN_DIM) for h in heads]
        state = [state_scr[h] for h in heads]
        u = [u_scr[rows, hc] for hc in hcs]
        w16 = [w_scr[rows, hc] for hc in hcs]
        qd = [qd_scr[rows, hc] for hc in hcs]
        kt = [kt_scr[rows, hc] for hc in hcs]
        attn = [attn_scr[h, rows, :] for h in heads]
        z = [z_ref[rows, hc] for hc in hcs]
        s16 = [sh.astype(BF16) for sh in state]
        v_new = [(uh - _dot(wh, sh)).astype(BF16) for uh, wh, sh in zip(u, w16, s16)]
        o = [_dot(qh, sh) + _dot(ah, vn) for qh, sh, ah, vn in zip(qd, s16, attn, v_new)]
        new_state = [sh * decay_last[:, DN_HEADS + h:DN_HEADS + h + 1] + _dot_tn(kth, vn)
                     for h, sh, kth, vn in zip(heads, state, kt, v_new)]
        o = [oh * lax.rsqrt(jnp.mean(oh * oh, axis=-1, keepdims=True) + EPS) * ognorm_ref[...] for oh in o]
        o = [(oh * _silu(zh.astype(F32))).astype(BF16) for oh, zh in zip(o, z)]
        for h in heads:
            state_scr[h] = new_state[h]
        for h in heads:
            o_ref[rows, hcs[h]] = o[h]
        return carry

    lax.fori_loop(0, n_chunks, scan_body, 0)


def _deltanet(proj3, ab3, conv_w, a_log_l, dtb_l, og_norm, tile):
    b, s, w = proj3.shape
    lcum = jnp.tril(jnp.ones((tile, tile), F32))
    same_chunk = (jnp.arange(tile)[:, None] // CHUNK) == (jnp.arange(tile)[None, :] // CHUNK)
    lcum = jnp.where(same_chunk, lcum, 0.0).astype(BF16)
    return pl.pallas_call(
        functools.partial(_deltanet_kernel, tile=tile),
        grid=(b, s // tile),
        in_specs=[
            pl.BlockSpec((None, tile, 3 * DN_WIDTH), lambda bi, t: (bi, t, COL_B // (3 * DN_WIDTH))),
            pl.BlockSpec((None, tile, DN_WIDTH), lambda bi, t: (bi, t, COL_Z // DN_WIDTH)),
            pl.BlockSpec((None, tile, LANES), lambda bi, t: (bi, t, 0)),
            pl.BlockSpec((CONV_WIDTH, 3 * DN_WIDTH), lambda bi, t: (0, 0)),
            pl.BlockSpec((1, LANES), lambda bi, t: (0, 0)),
            pl.BlockSpec((1, LANES), lambda bi, t: (0, 0)),
            pl.BlockSpec((1, DN_DIM), lambda bi, t: (0, 0)),
            pl.BlockSpec((tile, tile), lambda bi, t: (0, 0)),
        ],
        out_specs=pl.BlockSpec((None, tile, DN_WIDTH), lambda bi, t: (bi, t, 0)),
        out_shape=jax.ShapeDtypeStruct((b, s, DN_WIDTH), BF16),
        scratch_shapes=[
            pltpu.VMEM((tile, 3 * DN_WIDTH), F32),
            pltpu.VMEM((tile, LANES), F32),
            pltpu.VMEM((tile, LANES), F32),
            pltpu.VMEM((8, 3 * DN_WIDTH), F32),
            pltpu.VMEM((DN_HEADS, DN_DIM, DN_DIM), F32),
            pltpu.VMEM((3 * DN_HEADS, 8 + tile, DN_DIM), F32),
            pltpu.VMEM((tile, DN_WIDTH), F32),
            pltpu.VMEM((tile, DN_WIDTH), BF16),
            pltpu.VMEM((tile, DN_WIDTH), BF16),
            pltpu.VMEM((tile, DN_WIDTH), BF16),
            pltpu.VMEM((DN_HEADS, tile, CHUNK), BF16),
        ],
        compiler_params=_params(("parallel", "arbitrary")),
        name="deltanet",
    )(proj3, proj3, ab3, conv_w, a_log_l, dtb_l, og_norm, lcum)


def _merge_kernel(o0_ref, o1_ref, o2_ref, l0_ref, l1_ref, l2_ref, ob_ref, ga_ref, gb_ref, x_ref,
                  wa_ref, wb_ref, wo_ref, hexp_ref, gain_ref, wr_ref, rbias_ref,
                  x1_ref, h2_ref, logit_ref, o1_scr, o2_scr, l1_scr, l2_scr, *, tm):
    for g, src, dst in ((1, o1_ref, o1_scr), (2, o2_ref, o2_scr), (1, l1_ref, l1_scr), (2, l2_ref, l2_scr)):
        dil = ATTN_GROUPS[g][1]
        for r in range(dil):
            for c in range(dst.shape[0]):
                dst[c, pl.ds(r, tm // dil, stride=dil), :] = src[r, :, c * LANES:(c + 1) * LANES].astype(F32)

    def slabs(scr):
        return jnp.concatenate([scr[c] for c in range(scr.shape[0])], axis=-1)

    l0, l1, l2 = l0_ref[...], l1_scr[0], l2_scr[0]
    m = jnp.maximum(jnp.maximum(l0, l1), l2)
    e0, e1, e2 = jnp.exp(l0 - m), jnp.exp(l1 - m), jnp.exp(l2 - m)
    tot = e0 + e1 + e2
    hexp = hexp_ref[...]
    o_a = jnp.zeros(o0_ref.shape, F32)
    for e, o_val in ((e0, o0_ref[...].astype(F32)), (e1, slabs(o1_scr)), (e2, slabs(o2_scr))):
        w_hi, w_lo = _split2(e / tot)
        o_a = o_a + _dot(jnp.concatenate([w_hi, w_lo], axis=1), hexp) * o_val
    ya = _dot(o_a.astype(BF16), wa_ref[...])
    yb = _dot(ob_ref[...], wb_ref[...])
    merged = _sigmoid(ga_ref[...].astype(F32)) * ya + _sigmoid(gb_ref[...].astype(F32)) * yb
    x1 = x_ref[...] + _dot(merged.astype(BF16), wo_ref[...])
    x1_ref[...] = x1
    h2 = x1 * lax.rsqrt(jnp.mean(x1 * x1, axis=-1, keepdims=True) + EPS) * gain_ref[...]
    hi, lo = _split2(h2)
    h2_ref[...] = hi
    hi_terms = _dot(hi, wr_ref[...])
    logit_ref[...] = (hi_terms[:, :LANES] + hi_terms[:, LANES:] + _dot(lo, wr_ref[:, :LANES])
                      + rbias_ref[...])


def _merge(o0, l0, o_perm, l_perm, o_b, proj, x2, wa, wb, wo, gain, wr, rbias, s, tm):
    t, d = x2.shape
    tps = s // tm
    head_of_col = jnp.arange(GROUP_WIDTH) // HEAD_DIM_A
    hexp = (jnp.arange(LANES)[:, None] == head_of_col[None, :]).astype(BF16)
    hexp = jnp.concatenate([hexp, hexp], axis=0)

    def row(wd, cblk=0):
        return pl.BlockSpec((tm, wd), lambda i: (i, cblk))

    def perm(arr):
        _, dil, _, wd = arr.shape
        return pl.BlockSpec((None, dil, tm // dil, wd), lambda i: (i // tps, 0, i % tps, 0))

    def full(a):
        return pl.BlockSpec(a.shape, lambda i: (0, 0))

    return pl.pallas_call(
        functools.partial(_merge_kernel, tm=tm),
        grid=(t // tm,),
        in_specs=[row(GROUP_WIDTH), perm(o_perm[0]), perm(o_perm[1]),
                  row(LANES), perm(l_perm[0]), perm(l_perm[1]),
                  row(DN_WIDTH), row(d, COL_GA // d), row(d, COL_GB // d), row(d),
                  full(wa), full(wb), full(wo), full(hexp), full(gain), full(wr),
                  full(rbias)],
        out_specs=[row(d), row(d), row(LANES)],
        out_shape=[
            jax.ShapeDtypeStruct((t, d), F32),
            jax.ShapeDtypeStruct((t, d), BF16),
            jax.ShapeDtypeStruct((t, LANES), F32),
        ],
        scratch_shapes=[pltpu.VMEM((GROUP_WIDTH // LANES, tm, LANES), F32),
                        pltpu.VMEM((GROUP_WIDTH // LANES, tm, LANES), F32),
                        pltpu.VMEM((1, tm, LANES), F32), pltpu.VMEM((1, tm, LANES), F32)],
        compiler_params=_params(("parallel",)),
        name="merge",
    )(o0, o_perm[0], o_perm[1], l0, l_perm[0], l_perm[1], o_b, proj, proj, x2,
      wa, wb, wo, hexp, gain, wr, rbias)


def _routing_weights(logits):
    lane_i = lax.broadcasted_iota(jnp.int32, logits.shape, 1)
    lane = lane_i.astype(F32)
    neg_inf = jnp.float32(-jnp.inf)
    big = jnp.float32(1 << 20)
    is_group = lane_i < N_EXPERT_GROUPS
    gl = jnp.where(is_group, logits, neg_inf)
    gmax = jnp.max(gl, axis=-1, keepdims=True)
    gsum = jnp.sum(jnp.exp(gl - gmax), axis=-1, keepdims=True)
    p_g = 1.0 / gsum
    g_idx = jnp.min(jnp.where(gl == gmax, lane, big), axis=-1, keepdims=True)
    n_exp = N_EXPERT_GROUPS * EXPERTS_PER_GROUP
    is_exp = jnp.logical_and(lane_i >= N_EXPERT_GROUPS, lane_i < N_EXPERT_GROUPS + n_exp)
    exp_group = jnp.right_shift(lane_i - N_EXPERT_GROUPS, 2).astype(F32)
    assert EXPERTS_PER_GROUP == 4
    sel = jnp.logical_and(is_exp, exp_group == g_idx)
    el = jnp.where(sel, logits, neg_inf)
    emax = jnp.max(el, axis=-1, keepdims=True)
    ee = jnp.exp(el - emax)
    esum = jnp.sum(ee, axis=-1, keepdims=True)
    idx1 = jnp.min(jnp.where(el == emax, lane, big), axis=-1, keepdims=True)
    el2 = jnp.where(lane == idx1, neg_inf, el)
    e2max = jnp.max(el2, axis=-1, keepdims=True)
    idx2 = jnp.min(jnp.where(el2 == e2max, lane, big), axis=-1, keepdims=True)
    p1 = 1.0 / esum
    p2 = jnp.exp(e2max - emax) / esum
    top_sum = p1 + p2
    w = jnp.where(lane == idx1, p1 / top_sum, jnp.where(lane == idx2, p2 / top_sum, 0.0))
    return w * p_g, g_idx


def _moe_kernel(h2_ref, logit_ref, x1_ref, wg_ref, wu_ref, wd_ref, lstrict_ref, out_ref,
                hs_scr, cws_scr, pos_scr, acc_scr, off_smem, *, tm, sub):
    g = pl.program_id(1)
    lane = lax.broadcasted_iota(jnp.int32, (tm, LANES), 1)

    @pl.when(g == 0)
    def _():
        cw, g_idx = _routing_weights(logit_ref[...])
        onehot = jnp.where(lane.astype(F32) == g_idx, 1.0, 0.0)
        earlier = _dot(lstrict_ref[...], onehot.astype(BF16))
        counts = jnp.sum(onehot, axis=0, keepdims=True)
        lane_row = lax.broadcasted_iota(jnp.int32, (1, LANES), 1)
        start = jnp.zeros((1, LANES), F32)
        for k in range(N_EXPERT_GROUPS - 1):
            start = start + jnp.where(lane_row > k, counts[:, k:k + 1], 0.0)
        pos = jnp.sum(onehot * (earlier + start), axis=-1, keepdims=True)
        pos_b = jnp.broadcast_to(pos, (tm, LANES))
        pos_scr[...] = pos_b
        pos_row = jnp.transpose(pos_b)[0:1, :]
        dest = lax.broadcasted_iota(jnp.int32, (tm, tm), 0).astype(F32)
        gather = jnp.where(dest == pos_row, 1.0, 0.0).astype(BF16)
        hs_scr[...] = _dot(gather, h2_ref[...]).astype(BF16)
        cw_hi, cw_lo = _split2(cw)
        cws_scr[...] = _dot(gather, cw_hi) + _dot(gather, cw_lo)
        for k in range(N_EXPERT_GROUPS):
            off_smem[k] = start[0, k].astype(jnp.int32)
        off_smem[N_EXPERT_GROUPS] = jnp.int32(tm)
        acc_scr[...] = jnp.zeros_like(acc_scr)

    first = off_smem[g]
    last = off_smem[g + 1]
    lane_s = lax.broadcasted_iota(jnp.int32, (sub, LANES), 1)
    for s in range(tm // sub):
        @pl.when(jnp.logical_and(first < (s + 1) * sub, last > s * sub))
        def _(s=s):
            rows = slice(s * sub, (s + 1) * sub)
            hs = hs_scr[rows, :]
            hg = _dot(hs, wg_ref[...])
            hu = _dot(hs, wu_ref[...])
            cw = cws_scr[rows, :]
            parts = []
            for j in range(EXPERTS_PER_GROUP):
                cols = slice(j * D_EXPERT, (j + 1) * D_EXPERT)
                want = N_EXPERT_GROUPS + g * EXPERTS_PER_GROUP + j
                c_col = jnp.sum(jnp.where(lane_s == want, cw, 0.0), axis=-1, keepdims=True)
                parts.append((_silu(hg[:, cols]) * hu[:, cols] * c_col).astype(BF16))
            act = jnp.concatenate(parts, axis=-1)
            acc_scr[rows, :] += _dot(act, wd_ref[...])

    @pl.when(g == N_EXPERT_GROUPS - 1)
    def _():
        src = lax.broadcasted_iota(jnp.int32, (tm, tm), 1).astype(F32)
        scatter = jnp.where(pos_scr[:, 0:1] == src, 1.0, 0.0).astype(BF16)
        out_ref[...] = x1_ref[...] + _dot(scatter, acc_scr[...].astype(BF16))


def _moe(h2, logits, x1, wg, wu, wd, tm, sub):
    t, d = x1.shape
    gw = EXPERTS_PER_GROUP * D_EXPERT
    lstrict = jnp.tril(jnp.ones((tm, tm), F32), -1).astype(BF16)
    return pl.pallas_call(
        functools.partial(_moe_kernel, tm=tm, sub=sub),
        grid=(t // tm, N_EXPERT_GROUPS),
        in_specs=[
            pl.BlockSpec((tm, d), lambda i, g: (i, 0)),
            pl.BlockSpec((tm, LANES), lambda i, g: (i, 0)),
            pl.BlockSpec((tm, d), lambda i, g: (i, 0)),
            pl.BlockSpec((None, d, gw), lambda i, g: (g, 0, 0)),
            pl.BlockSpec((None, d, gw), lambda i, g: (g, 0, 0)),
            pl.BlockSpec((None, gw, d), lambda i, g: (g, 0, 0)),
            pl.BlockSpec((tm, tm), lambda i, g: (0, 0)),
        ],
        out_specs=pl.BlockSpec((tm, d), lambda i, g: (i, 0)),
        out_shape=jax.ShapeDtypeStruct((t, d), F32),
        scratch_shapes=[pltpu.VMEM((tm, d), BF16), pltpu.VMEM((tm, LANES), F32),
                        pltpu.VMEM((tm, LANES), F32), pltpu.VMEM((tm, d), F32),
                        pltpu.SMEM((8,), jnp.int32)],
        compiler_params=_params(("parallel", "arbitrary")),
        name="moe",
    )(h2, logits, x1, wg, wu, wd, lstrict)


def _ple_kernel(x_ref, p_ref, gain_ref, wpg_ref, wple_ref, out_ref):
    x = x_ref[...]
    h3 = x * lax.rsqrt(jnp.mean(x * x, axis=-1, keepdims=True) + EPS) * gain_ref[...]
    gate = _sigmoid(_dot(h3.astype(BF16), wpg_ref[...]))
    ple = _dot(p_ref[...].astype(BF16), wple_ref[...])
    out_ref[...] = x + gate * ple


def _ple(x2, p2, gain, wpg, wple, tm):
    t, d = x2.shape
    pd = p2.shape[1]
    return pl.pallas_call(
        _ple_kernel,
        grid=(t // tm,),
        in_specs=[
            pl.BlockSpec((tm, d), lambda i: (i, 0)),
            pl.BlockSpec((tm, pd), lambda i: (i, 0)),
            pl.BlockSpec((1, d), lambda i: (0, 0)),
            pl.BlockSpec((d, d), lambda i: (0, 0)),
            pl.BlockSpec((pd, d), lambda i: (0, 0)),
        ],
        out_specs=pl.BlockSpec((tm, d), lambda i: (i, 0)),
        out_shape=jax.ShapeDtypeStruct((t, d), F32),
        compiler_params=_params(("parallel",)),
        name="ple",
    )(x2, p2, gain, wpg, wple)


def _lane_pad(v, offset):
    out = jnp.zeros((1, LANES), F32)
    return out.at[0, offset:offset + v.shape[0]].set(v.astype(F32))


def _layer(x2, p2, b, s, norm_mix, w_in, q_norm, k_norm, conv_w, a_log, dt_bias, dn_out_norm,
           w_branch_a, w_branch_b, w_out, norm_ffn, w_router_group, b_router_group,
           w_router_expert, b_router_expert, w_expert_gate, w_expert_up, w_expert_down,
           norm_ple, w_ple, w_ple_gate):
    t, d = x2.shape
    tiles = _tiles(s)
    qa = N_GROUPS * GROUP_WIDTH
    o_b0 = 3 * qa
    o_z = o_b0 + 3 * DN_WIDTH
    o_beta = o_z + DN_WIDTH
    o_alpha = o_beta + DN_HEADS
    o_ga = o_alpha + DN_HEADS
    o_gb = o_ga + d
    w_in = w_in.astype(F32)
    gw_a = GROUP_WIDTH
    w_groups = [jnp.concatenate([w_in[:, part * qa + g * gw_a:part * qa + (g + 1) * gw_a]
                                 for part in range(3)], axis=1).astype(BF16) for g in range(N_GROUPS)]
    w_plain = jnp.concatenate([
        w_in[:, o_b0:o_z],
        w_in[:, o_z:o_beta],
        w_in[:, o_ga:o_gb],
        w_in[:, o_gb:o_gb + d],
    ], axis=1).astype(BF16)
    w_ab = jnp.zeros((d, LANES), F32).at[:, 0:2 * DN_HEADS].set(w_in[:, o_beta:o_ga])
    wab = jnp.concatenate(_split2(w_ab), axis=1)
    qg = jnp.tile(q_norm.astype(F32), HEADS_A) * (HEAD_DIM_A ** -0.5 * LOG2E)
    kg = jnp.tile(k_norm.astype(F32), HEADS_A)
    qkgain = jnp.concatenate([qg, kg])[None, :]

    proj, ab, qkv = _in_projection(x2, norm_mix.astype(F32)[None, :], w_plain, w_groups, wab,
                                   qkgain, b, s, tm=tiles["in_proj"])
    proj3 = proj.reshape(b, s, PROJ_WIDTH)

    (o0, l0), (o1, l1), (o2, l2) = [_attention_group(a, 0, 1, 2, tq=tiles["attn"]) for a in qkv]

    o_b = _deltanet(proj3, ab.reshape(b, s, LANES), conv_w.astype(F32),
                    _lane_pad(a_log, DN_HEADS), _lane_pad(dt_bias, DN_HEADS),
                    dn_out_norm.astype(F32)[None, :], tile=tiles["deltanet"]).reshape(t, DN_WIDTH)

    w_r = jnp.zeros((d, LANES), F32)
    w_r = w_r.at[:, 0:N_EXPERT_GROUPS].set(w_router_group.astype(F32))
    n_exp = N_EXPERT_GROUPS * EXPERTS_PER_GROUP
    w_r = w_r.at[:, N_EXPERT_GROUPS:N_EXPERT_GROUPS + n_exp].set(w_router_expert.astype(F32))
    w_r2 = jnp.concatenate(_split2(w_r), axis=1)
    rbias = _lane_pad(jnp.concatenate([b_router_group.astype(F32), b_router_expert.astype(F32)]), 0)

    x1, h2, logits = _merge(o0.reshape(t, GROUP_WIDTH), l0.reshape(t, LANES), (o1, o2), (l1, l2), o_b,
                            proj, x2, w_branch_a.astype(BF16), w_branch_b.astype(BF16),
                            w_out.astype(BF16), norm_ffn.astype(F32)[None, :], w_r2, rbias,
                            s, tm=tiles["merge"])

    gw = EXPERTS_PER_GROUP * D_EXPERT
    wg = jnp.transpose(w_expert_gate, (0, 2, 1, 3)).reshape(N_EXPERT_GROUPS, d, gw).astype(BF16)
    wu = jnp.transpose(w_expert_up, (0, 2, 1, 3)).reshape(N_EXPERT_GROUPS, d, gw).astype(BF16)
    wd = w_expert_down.reshape(N_EXPERT_GROUPS, gw, d).astype(BF16)
    x2n = _moe(h2, logits, x1, wg, wu, wd, tm=tiles["moe"], sub=tiles["moe_slab"])

    return _ple(x2n, p2, norm_ple.astype(F32)[None, :], w_ple_gate.astype(BF16), w_ple.astype(BF16),
                tm=tiles["ple"])


def kernel(x, p, norm_mix, w_in, q_norm, k_norm, conv_w, a_log, dt_bias, dn_out_norm, w_branch_a, w_branch_b, w_out, norm_ffn, w_router_group, b_router_group, w_router_expert, b_router_expert, w_expert_gate, w_expert_up, w_expert_down, norm_ple, w_ple, w_ple_gate):
    b, s, d = x.shape
    depth = w_in.shape[0]
    x2 = x.astype(F32).reshape(b * s, d)
    for i in range(depth):
        x2 = _layer(x2, p[i].reshape(b * s, -1), b, s, norm_mix[i], w_in[i], q_norm[i], k_norm[i],
                    conv_w[i], a_log[i], dt_bias[i], dn_out_norm[i], w_branch_a[i], w_branch_b[i],
                    w_out[i], norm_ffn[i], w_router_group[i], b_router_group[i], w_router_expert[i],
                    b_router_expert[i], w_expert_gate[i], w_expert_up[i], w_expert_down[i],
                    norm_ple[i], w_ple[i], w_ple_gate[i])
    return x2.reshape(b, s, d)
```

```python
import functools

import jax
import jax.numpy as jnp
from jax import lax
from jax.experimental import pallas as pl
from jax.experimental.pallas import tpu as pltpu

F32 = jnp.float32
BF16 = jnp.bfloat16
EPS = 1e-6
LN2 = 0.6931471805599453
LOG2E = 1.4426950408889634

ATTN_GROUPS = ((128, 1), (512, 4), (2048, 16))
N_GROUPS = 3
HEADS_A = 8
HEAD_DIM_A = 64
GROUP_WIDTH = HEADS_A * HEAD_DIM_A
ATTN_BLOCK = 128
DN_HEADS = 8
DN_DIM = 128
DN_WIDTH = DN_HEADS * DN_DIM
CONV_WIDTH = 4
CHUNK = 64
N_EXPERT_GROUPS = 4
EXPERTS_PER_GROUP = 4
D_EXPERT = 256
SUBLANES = 8
LANES = 128

PROJ_TILE = 1536
COL_B, COL_Z, COL_GA, COL_GB = 0, 3072, 4096, 5120
PROJ_WIDTH = 6144

V7X_VMEM_BYTES = 64 * 1024 * 1024
VMEM_LIMIT = V7X_VMEM_BYTES * 7 // 8


def _tiles(s):
    return dict(
        in_proj=min(1024, s),
        attn=512,
        deltanet=min(512, s),
        merge=min(512, s),
        moe=min(1024, s),
        moe_slab=256,
        ple=min(1024, s),
    )


def _dot(a, b):
    return jnp.dot(a, b, preferred_element_type=F32)


def _dot_nt(a, b):
    return lax.dot_general(a, b, (((1,), (1,)), ((), ())), preferred_element_type=F32)


def _dot_tn(a, b):
    return lax.dot_general(a, b, (((0,), (0,)), ((), ())), preferred_element_type=F32)


def _split2(v):
    hi = v.astype(BF16)
    lo = (v - hi.astype(F32)).astype(BF16)
    return hi, lo


def _split3(v):
    a = v.astype(BF16)
    r = v - a.astype(F32)
    b = r.astype(BF16)
    c = (r - b.astype(F32)).astype(BF16)
    return a, b, c


def _sigmoid(v):
    return 0.5 + 0.5 * jnp.tanh(0.5 * v)


def _silu(v):
    half = 0.5 * v
    return half + half * jnp.tanh(half)


def _heads_to_lanes(per_head):
    lane = lax.broadcasted_iota(jnp.int32, per_head.shape, 1)
    heads_per_slab = LANES // HEAD_DIM_A
    outs = []
    for c in range(GROUP_WIDTH // LANES):
        idx = lane // HEAD_DIM_A + heads_per_slab * c
        outs.append(jnp.take_along_axis(per_head, idx, axis=1))
    return jnp.concatenate(outs, axis=1)


def _params(sem):
    return pltpu.CompilerParams(dimension_semantics=sem, vmem_limit_bytes=VMEM_LIMIT)


def _inproj_main_kernel(x_ref, gain_ref, w_ref, wab_ref, proj_ref, ab_ref, h_ref):
    @pl.when(pl.program_id(1) == 0)
    def _():
        x = x_ref[...]
        ms = jnp.mean(x * x, axis=-1, keepdims=True)
        h = x * lax.rsqrt(ms + EPS) * gain_ref[...]
        hi, lo = _split2(h)
        h_ref[...] = hi
        hi_terms = _dot(hi, wab_ref[...])
        ab_ref[...] = hi_terms[:, :LANES] + hi_terms[:, LANES:] + _dot(lo, wab_ref[:, :LANES])

    proj_ref[...] = _dot(h_ref[...], w_ref[...]).astype(BF16)


def _inproj_attn_kernel(h_ref, w_ref, qkgain_ref, red_ref, out_ref, perm_scr, *, tm, g):
    slabs = GROUP_WIDTH // LANES

    def normed(a, gain):
        ss = _dot((a * a).astype(BF16), red_ref[...])
        r = lax.rsqrt(ss * (1.0 / HEAD_DIM_A) + EPS)
        return a * _heads_to_lanes(r) * gain

    h = h_ref[...]
    dil = ATTN_GROUPS[g][1]
    n_u = tm // dil
    for p in range(3):
        pcols = slice(p * GROUP_WIDTH, (p + 1) * GROUP_WIDTH)
        val = _dot(h, w_ref[:, pcols])
        if p < 2:
            val = normed(val, qkgain_ref[:, pcols])
        if dil == 1:
            out_ref[0, :, pcols] = val.astype(BF16)
        else:
            if dil < SUBLANES:
                for c in range(slabs):
                    perm_scr[c, 0:tm, :] = val[:, c * LANES:(c + 1) * LANES]
                for r in range(dil):
                    for c in range(slabs):
                        lo = p * GROUP_WIDTH + c * LANES
                        out_ref[r, :, lo:lo + LANES] = (
                            perm_scr[c, pl.ds(r, n_u, stride=dil), :].astype(BF16))
            else:
                pitch = n_u + 1
                for c in range(slabs):
                    for m in range(tm // SUBLANES):
                        u, r0 = divmod(m * SUBLANES, dil)
                        perm_scr[c, pl.ds(r0 * pitch + u, SUBLANES, stride=pitch), :] = (
                            val[m * SUBLANES:(m + 1) * SUBLANES, c * LANES:(c + 1) * LANES])
                for r in range(dil):
                    for c in range(slabs):
                        lo = p * GROUP_WIDTH + c * LANES
                        out_ref[r, :, lo:lo + LANES] = perm_scr[c, pl.ds(r * pitch, n_u), :].astype(BF16)


def _in_projection(x2, norm_gain, w_plain, w_groups, wab, qkgain, b, s, tm):
    t, d = x2.shape
    assert w_plain.shape[1] == PROJ_WIDTH and PROJ_WIDTH % PROJ_TILE == 0
    proj, ab, h = pl.pallas_call(
        _inproj_main_kernel,
        grid=(t // tm, PROJ_WIDTH // PROJ_TILE),
        in_specs=[
            pl.BlockSpec((tm, d), lambda i, j: (i, 0)),
            pl.BlockSpec((1, d), lambda i, j: (0, 0)),
            pl.BlockSpec((d, PROJ_TILE), lambda i, j: (0, j)),
            pl.BlockSpec((d, 2 * LANES), lambda i, j: (0, 0)),
        ],
        out_specs=[
            pl.BlockSpec((tm, PROJ_TILE), lambda i, j: (i, j)),
            pl.BlockSpec((tm, LANES), lambda i, j: (i, 0)),
            pl.BlockSpec((tm, d), lambda i, j: (i, 0)),
        ],
        out_shape=[
            jax.ShapeDtypeStruct((t, PROJ_WIDTH), BF16),
            jax.ShapeDtypeStruct((t, LANES), F32),
            jax.ShapeDtypeStruct((t, d), BF16),
        ],
        compiler_params=_params(("parallel", "arbitrary")),
        name="in_proj",
    )(x2, norm_gain, w_plain, wab)

    tps = s // tm
    col = jnp.arange(GROUP_WIDTH) // HEAD_DIM_A
    red = (col[:, None] == jnp.arange(LANES)[None, :]).astype(BF16)
    qkv = []
    for g, (_, dil) in enumerate(ATTN_GROUPS):
        qkv.append(pl.pallas_call(
            functools.partial(_inproj_attn_kernel, tm=tm, g=g),
            grid=(t // tm,),
            in_specs=[
                pl.BlockSpec((tm, d), lambda i: (i, 0)),
                pl.BlockSpec((d, PROJ_TILE), lambda i: (0, 0)),
                pl.BlockSpec((1, 2 * GROUP_WIDTH), lambda i: (0, 0)),
                pl.BlockSpec((GROUP_WIDTH, LANES), lambda i: (0, 0)),
            ],
            out_specs=pl.BlockSpec((None, dil, tm // dil, PROJ_TILE), lambda i: (i // tps, 0, i % tps, 0)),
            out_shape=jax.ShapeDtypeStruct((b, dil, s // dil, PROJ_TILE), BF16),
            scratch_shapes=[pltpu.VMEM((GROUP_WIDTH // LANES, tm + dil, LANES), F32)],
            compiler_params=_params(("parallel",)),
            name=f"in_proj_attn_d{dil}",
        )(h, w_groups[g], qkgain, red))
    return proj, ab, qkv


def _attn_kernel(q_ref, kc_ref, kp_ref, vc_ref, vp_ref, o_ref, lse_ref, *, n_blk, pairs_at_once=4):
    n = pl.program_id(2)
    blk = ATTN_BLOCK
    qi = lax.broadcasted_iota(jnp.int32, (2 * blk, blk), 0) & (blk - 1)
    kj = lax.broadcasted_iota(jnp.int32, (2 * blk, blk), 1)
    cur_ok2 = kj <= qi
    prev_band = kj >= qi
    lane = lax.broadcasted_iota(jnp.int32, (blk, LANES), 1)
    low_half = lane < HEAD_DIM_A
    neg_inf = jnp.float32(-jnp.inf)
    ones = jnp.ones((blk, LANES), BF16)
    zero = jnp.zeros((blk, LANES), BF16)

    for b in range(n_blk):
        rows = slice(b * blk, (b + 1) * blk)
        if b == 0:
            prev_ok2 = jnp.logical_and(prev_band, n > 0)
        else:
            prev_ok2 = prev_band
        lse_tile = jnp.zeros((blk, LANES), F32)
        for p0 in range(0, HEADS_A // 2, pairs_at_once):
            pairs = range(p0, p0 + pairs_at_once)
            cols = [slice(pr * LANES, (pr + 1) * LANES) for pr in pairs]
            q2 = [q_ref[rows, c] for c in cols]
            kc2 = [kc_ref[rows, c] for c in cols]
            vc2 = [vc_ref[rows, c] for c in cols]
            if b == 0:
                kp2 = [kp_ref[:, c] for c in cols]
                vp2 = [vp_ref[:, c] for c in cols]
            else:
                prow = slice((b - 1) * blk, b * blk)
                kp2 = [kc_ref[prow, c] for c in cols]
                vp2 = [vc_ref[prow, c] for c in cols]
            q_st = [jnp.concatenate([jnp.where(low_half, q, zero), jnp.where(low_half, zero, q)], axis=0)
                    for q in q2]
            vc_aug = [jnp.concatenate([x, ones], axis=1) for x in vc2]
            vp_aug = [jnp.concatenate([x, ones], axis=1) for x in vp2]
            s_c = [jnp.where(cur_ok2, _dot_nt(q, kk), neg_inf) for q, kk in zip(q_st, kc2)]
            s_p = [jnp.where(prev_ok2, _dot_nt(q, kk), neg_inf) for q, kk in zip(q_st, kp2)]
            m = [jnp.maximum(jnp.max(sc, axis=-1, keepdims=True), jnp.max(sp, axis=-1, keepdims=True))
                 for sc, sp in zip(s_c, s_p)]
            e_c = [jnp.exp2(sc - mh).astype(BF16) for sc, mh in zip(s_c, m)]
            e_p = [jnp.exp2(sp - mh).astype(BF16) for sp, mh in zip(s_p, m)]
            nd = [_dot(ec, va) + _dot(ep, vb) for ec, va, ep, vb in zip(e_c, vc_aug, e_p, vp_aug)]
            outs = [x[:, :LANES] / x[:, LANES:] for x in nd]
            lse = [mh * LN2 + jnp.log(x[:, LANES:LANES + 1]) for mh, x in zip(m, nd)]
            for i, pr in enumerate(pairs):
                for half in range(2):
                    hrows = slice(half * blk, (half + 1) * blk)
                    lse_tile = jnp.where(lane == 2 * pr + half, lse[i][hrows], lse_tile)
                o_ref[rows, cols[i]] = jnp.where(low_half, outs[i][0:blk], outs[i][blk:2 * blk]).astype(BF16)
        lse_ref[rows, :] = lse_tile


def _attention_group(arr4, cq, ck, cv, tq):
    b, dil, n_sub, _ = arr4.shape
    tq = min(tq, n_sub)
    n_blk = tq // ATTN_BLOCK

    def cur(c0):
        return pl.BlockSpec((None, None, tq, GROUP_WIDTH), lambda bi, r, n: (bi, r, n, c0))

    def prev(c0):
        return pl.BlockSpec((None, None, ATTN_BLOCK, GROUP_WIDTH),
                            lambda bi, r, n: (bi, r, jnp.maximum(n * n_blk - 1, 0), c0))

    return pl.pallas_call(
        functools.partial(_attn_kernel, n_blk=n_blk),
        grid=(b, dil, n_sub // tq),
        in_specs=[cur(cq), cur(ck), prev(ck), cur(cv), prev(cv)],
        out_specs=[
            pl.BlockSpec((None, None, tq, GROUP_WIDTH), lambda bi, r, n: (bi, r, n, 0)),
            pl.BlockSpec((None, None, tq, LANES), lambda bi, r, n: (bi, r, n, 0)),
        ],
        out_shape=[
            jax.ShapeDtypeStruct((b, dil, n_sub, GROUP_WIDTH), BF16),
            jax.ShapeDtypeStruct((b, dil, n_sub, LANES), F32),
        ],
        compiler_params=_params(("parallel", "parallel", "parallel")),
        name=f"attn_d{dil}",
    )(arr4, arr4, arr4, arr4, arr4)


def _deltanet_kernel(xb_ref, z_ref, ab_ref, convw_ref, alog_ref, dtb_ref, ognorm_ref, lcum_ref,
                     o_ref, qkv_scr, gc_scr, beta_scr, carry_scr, state_scr, ext_scr,
                     u_scr, w_scr, qd_scr, kt_scr, attn_scr, *, tile, chunks_per_iter=8):
    t_idx = pl.program_id(1)
    n_chunks = tile // CHUNK
    halo = 8
    heads = range(DN_HEADS)

    @pl.when(t_idx == 0)
    def _():
        carry_scr[...] = jnp.zeros_like(carry_scr)
        state_scr[...] = jnp.zeros_like(state_scr)

    for c in range(3 * DN_HEADS):
        cols = slice(c * DN_DIM, (c + 1) * DN_DIM)
        cur = xb_ref[:, cols].astype(F32)
        ext_scr[c, 0:halo, :] = carry_scr[:, cols]
        ext_scr[c, halo:halo + tile, :] = cur
        y = cur * convw_ref[CONV_WIDTH - 1:CONV_WIDTH, cols]
        for j in range(CONV_WIDTH - 1):
            off = halo - (CONV_WIDTH - 1) + j
            y = y + ext_scr[c, pl.ds(off, tile), :] * convw_ref[j:j + 1, cols]
        carry_scr[:, cols] = cur[tile - halo:tile]
        y = _silu(y)
        if c < 2 * DN_HEADS:
            inv = lax.rsqrt(jnp.sum(y * y, axis=-1, keepdims=True) + EPS)
            if c < DN_HEADS:
                inv = inv * (DN_DIM ** -0.5)
            y = y * inv
        qkv_scr[:, cols] = y

    ab = ab_ref[...]
    beta_scr[...] = _sigmoid(ab)
    sp_in = ab + dtb_ref[...]
    softplus = jnp.maximum(sp_in, 0.0) + jnp.log(1.0 + jnp.exp(-jnp.abs(sp_in)))
    gdec = -jnp.exp(alog_ref[...]) * softplus
    g1, g2, g3 = _split3(gdec)
    lcum = lcum_ref[...]
    gc_scr[...] = _dot(lcum, g1) + _dot(lcum, g2) + _dot(lcum, g3)

    ri = lax.broadcasted_iota(jnp.int32, (CHUNK, CHUNK), 0)
    ci = lax.broadcasted_iota(jnp.int32, (CHUNK, CHUNK), 1)
    tril = ci <= ri
    tril_strict = ci < ri
    eye = (ci == ri).astype(F32)
    neg_inf = jnp.float32(-jnp.inf)

    def intra_body(c, carry):
        rows_c = [pl.ds(pl.multiple_of((c * chunks_per_iter + i) * CHUNK, CHUNK), CHUNK)
                  for i in range(chunks_per_iter)]
        units = [(i, h) for i in range(chunks_per_iter) for h in heads]
        gc_c = [gc_scr[r, :] for r in rows_c]
        gc_t_c = [jnp.transpose(g) for g in gc_c]
        beta_c = [beta_scr[r, :] for r in rows_c]
        q = [qkv_scr[rows_c[i], h * DN_DIM:(h + 1) * DN_DIM] for i, h in units]
        k = [qkv_scr[rows_c[i], DN_WIDTH + h * DN_DIM:DN_WIDTH + (h + 1) * DN_DIM] for i, h in units]
        v = [qkv_scr[rows_c[i], 2 * DN_WIDTH + h * DN_DIM:2 * DN_WIDTH + (h + 1) * DN_DIM] for i, h in units]

        b_col = [beta_c[i][:, h:h + 1] for i, h in units]
        g_col = [gc_c[i][:, DN_HEADS + h:DN_HEADS + h + 1] for i, h in units]
        g_row = [gc_t_c[i][DN_HEADS + h:DN_HEADS + h + 1, :] for i, h in units]
        g_last = [gcol[CHUNK - 1:CHUNK, :] for gcol in g_col]
        decay = [jnp.exp(jnp.where(tril, gcol - grow, neg_inf)) for gcol, grow in zip(g_col, g_row)]
        e_g = [jnp.exp(gcol) for gcol in g_col]
        kb = [kh * bh for kh, bh in zip(k, b_col)]
        k16 = [kh.astype(BF16) for kh in k]
        a_mat = [jnp.where(tril_strict, _dot_nt(kbh.astype(BF16), kh16) * dh, 0.0)
                 for kbh, kh16, dh in zip(kb, k16, decay)]
        pw = [-a for a in a_mat]
        t_mat = [eye + p for p in pw]
        for _ in range(5):
            pw16 = [p.astype(BF16) for p in pw]
            pw = [_dot(p, p) for p in pw16]
            t_mat = [th + _dot(th.astype(BF16), p.astype(BF16)) for th, p in zip(t_mat, pw)]
        t16 = [th.astype(BF16) for th in t_mat]
        u = [_dot(th, (vh * bh).astype(BF16)) for th, vh, bh in zip(t16, v, b_col)]
        w = [_dot(th, (kbh * eh).astype(BF16)) for th, kbh, eh in zip(t16, kb, e_g)]
        attn = [jnp.where(tril, _dot_nt(qh.astype(BF16), kh16) * dh, 0.0).astype(BF16)
                for qh, kh16, dh in zip(q, k16, decay)]
        q_dec = [(qh * eh).astype(BF16) for qh, eh in zip(q, e_g)]
        k_tail = [(kh * jnp.exp(gl - gcol)).astype(BF16) for kh, gl, gcol in zip(k, g_last, g_col)]
        for n, (i, h) in enumerate(units):
            hc = slice(h * DN_DIM, (h + 1) * DN_DIM)
            u_scr[rows_c[i], hc] = u[n]
            w_scr[rows_c[i], hc] = w[n].astype(BF16)
            qd_scr[rows_c[i], hc] = q_dec[n]
            kt_scr[rows_c[i], hc] = k_tail[n]
            attn_scr[h, rows_c[i], :] = attn[n]
        return carry

    lax.fori_loop(0, n_chunks // chunks_per_iter, intra_body, 0)

    def scan_body(c, carry):
        r0 = pl.multiple_of(c * CHUNK, CHUNK)
        rows = pl.ds(r0, CHUNK)
        decay_last = jnp.exp(gc_scr[pl.ds(r0 + CHUNK - 1, 1), :])
        hcs = [slice(h * DN_DIM, (h + 1) * DN_DIM) for h in heads]
        state = [state_scr[h] for h in heads]
        u = [u_scr[rows, hc] for hc in hcs]
        w16 = [w_scr[rows, hc] for hc in hcs]
        qd = [qd_scr[rows, hc] for hc in hcs]
        kt = [kt_scr[rows, hc] for hc in hcs]
        attn = [attn_scr[h, rows, :] for h in heads]
        z = [z_ref[rows, hc] for hc in hcs]
        s16 = [sh.astype(BF16) for sh in state]
        v_new = [(uh - _dot(wh, sh)).astype(BF16) for uh, wh, sh in zip(u, w16, s16)]
        o = [_dot(qh, sh) + _dot(ah, vn) for qh, sh, ah, vn in zip(qd, s16, attn, v_new)]
        new_state = [sh * decay_last[:, DN_HEADS + h:DN_HEADS + h + 1] + _dot_tn(kth, vn)
                     for h, sh, kth, vn in zip(heads, state, kt, v_new)]
        o = [oh * lax.rsqrt(jnp.mean(oh * oh, axis=-1, keepdims=True) + EPS) * ognorm_ref[...] for oh in o]
        o = [(oh * _silu(zh.astype(F32))).astype(BF16) for oh, zh in zip(o, z)]
        for h in heads:
            state_scr[h] = new_state[h]
        for h in heads:
            o_ref[rows, hcs[h]] = o[h]
        return carry

    lax.fori_loop(0, n_chunks, scan_body, 0)


def _deltanet(proj3, ab3, conv_w, a_log_l, dtb_l, og_norm, tile):
    b, s, w = proj3.shape
    lcum = jnp.tril(jnp.ones((tile, tile), F32))
    same_chunk = (jnp.arange(tile)[:, None] // CHUNK) == (jnp.arange(tile)[None, :] // CHUNK)
    lcum = jnp.where(same_chunk, lcum, 0.0).astype(BF16)
    return pl.pallas_call(
        functools.partial(_deltanet_kernel, tile=tile),
        grid=(b, s // tile),
        in_specs=[
            pl.BlockSpec((None, tile, 3 * DN_WIDTH), lambda bi, t: (bi, t, COL_B // (3 * DN_WIDTH))),
            pl.BlockSpec((None, tile, DN_WIDTH), lambda bi, t: (bi, t, COL_Z // DN_WIDTH)),
            pl.BlockSpec((None, tile, LANES), lambda bi, t: (bi, t, 0)),
            pl.BlockSpec((CONV_WIDTH, 3 * DN_WIDTH), lambda bi, t: (0, 0)),
            pl.BlockSpec((1, LANES), lambda bi, t: (0, 0)),
            pl.BlockSpec((1, LANES), lambda bi, t: (0, 0)),
            pl.BlockSpec((1, DN_DIM), lambda bi, t: (0, 0)),
            pl.BlockSpec((tile, tile), lambda bi, t: (0, 0)),
        ],
        out_specs=pl.BlockSpec((None, tile, DN_WIDTH), lambda bi, t: (bi, t, 0)),
        out_shape=jax.ShapeDtypeStruct((b, s, DN_WIDTH), BF16),
        scratch_shapes=[
            pltpu.VMEM((tile, 3 * DN_WIDTH), F32),
            pltpu.VMEM((tile, LANES), F32),
            pltpu.VMEM((tile, LANES), F32),
            pltpu.VMEM((8, 3 * DN_WIDTH), F32),
            pltpu.VMEM((DN_HEADS, DN_DIM, DN_DIM), F32),
            pltpu.VMEM((3 * DN_HEADS, 8 + tile, DN_DIM), F32),
            pltpu.VMEM((tile, DN_WIDTH), F32),
            pltpu.VMEM((tile, DN_WIDTH), BF16),
            pltpu.VMEM((tile, DN_WIDTH), BF16),
            pltpu.VMEM((tile, DN_WIDTH), BF16),
            pltpu.VMEM((DN_HEADS, tile, CHUNK), BF16),
        ],
        compiler_params=_params(("parallel", "arbitrary")),
        name="deltanet",
    )(proj3, proj3, ab3, conv_w, a_log_l, dtb_l, og_norm, lcum)


def _merge_kernel(o0_ref, o1_ref, o2_ref, l0_ref, l1_ref, l2_ref, ob_ref, ga_ref, gb_ref, x_ref,
                  wa_ref, wb_ref, wo_ref, hexp_ref, gain_ref, wr_ref, rbias_ref,
                  x1_ref, h2_ref, logit_ref, o1_scr, o2_scr, l1_scr, l2_scr, *, tm):
    for g, src, dst in ((1, o1_ref, o1_scr), (2, o2_ref, o2_scr), (1, l1_ref, l1_scr), (2, l2_ref, l2_scr)):
        dil = ATTN_GROUPS[g][1]
        for r in range(dil):
            for c in range(dst.shape[0]):
                dst[c, pl.ds(r, tm // dil, stride=dil), :] = src[r, :, c * LANES:(c + 1) * LANES].astype(F32)

    def slabs(scr):
        return jnp.concatenate([scr[c] for c in range(scr.shape[0])], axis=-1)

    l0, l1, l2 = l0_ref[...], l1_scr[0], l2_scr[0]
    m = jnp.maximum(jnp.maximum(l0, l1), l2)
    e0, e1, e2 = jnp.exp(l0 - m), jnp.exp(l1 - m), jnp.exp(l2 - m)
    tot = e0 + e1 + e2
    hexp = hexp_ref[...]
    o_a = jnp.zeros(o0_ref.shape, F32)
    for e, o_val in ((e0, o0_ref[...].astype(F32)), (e1, slabs(o1_scr)), (e2, slabs(o2_scr))):
        w_hi, w_lo = _split2(e / tot)
        o_a = o_a + _dot(jnp.concatenate([w_hi, w_lo], axis=1), hexp) * o_val
    ya = _dot(o_a.astype(BF16), wa_ref[...])
    yb = _dot(ob_ref[...], wb_ref[...])
    merged = _sigmoid(ga_ref[...].astype(F32)) * ya + _sigmoid(gb_ref[...].astype(F32)) * yb
    x1 = x_ref[...] + _dot(merged.astype(BF16), wo_ref[...])
    x1_ref[...] = x1
    h2 = x1 * lax.rsqrt(jnp.mean(x1 * x1, axis=-1, keepdims=True) + EPS) * gain_ref[...]
    hi, lo = _split2(h2)
    h2_ref[...] = hi
    hi_terms = _dot(hi, wr_ref[...])
    logit_ref[...] = (hi_terms[:, :LANES] + hi_terms[:, LANES:] + _dot(lo, wr_ref[:, :LANES])
                      + rbias_ref[...])


def _merge(o0, l0, o_perm, l_perm, o_b, proj, x2, wa, wb, wo, gain, wr, rbias, s, tm):
    t, d = x2.shape
    tps = s // tm
    head_of_col = jnp.arange(GROUP_WIDTH) // HEAD_DIM_A
    hexp = (jnp.arange(LANES)[:, None] == head_of_col[None, :]).astype(BF16)
    hexp = jnp.concatenate([hexp, hexp], axis=0)

    def row(wd, cblk=0):
        return pl.BlockSpec((tm, wd), lambda i: (i, cblk))

    def perm(arr):
        _, dil, _, wd = arr.shape
        return pl.BlockSpec((None, dil, tm // dil, wd), lambda i: (i // tps, 0, i % tps, 0))

    def full(a):
        return pl.BlockSpec(a.shape, lambda i: (0, 0))

    return pl.pallas_call(
        functools.partial(_merge_kernel, tm=tm),
        grid=(t // tm,),
        in_specs=[row(GROUP_WIDTH), perm(o_perm[0]), perm(o_perm[1]),
                  row(LANES), perm(l_perm[0]), perm(l_perm[1]),
                  row(DN_WIDTH), row(d, COL_GA // d), row(d, COL_GB // d), row(d),
                  full(wa), full(wb), full(wo), full(hexp), full(gain), full(wr),
                  full(rbias)],
        out_specs=[row(d), row(d), row(LANES)],
        out_shape=[
            jax.ShapeDtypeStruct((t, d), F32),
            jax.ShapeDtypeStruct((t, d), BF16),
            jax.ShapeDtypeStruct((t, LANES), F32),
        ],
        scratch_shapes=[pltpu.VMEM((GROUP_WIDTH // LANES, tm, LANES), F32),
                        pltpu.VMEM((GROUP_WIDTH // LANES, tm, LANES), F32),
                        pltpu.VMEM((1, tm, LANES), F32), pltpu.VMEM((1, tm, LANES), F32)],
        compiler_params=_params(("parallel",)),
        name="merge",
    )(o0, o_perm[0], o_perm[1], l0, l_perm[0], l_perm[1], o_b, proj, proj, x2,
      wa, wb, wo, hexp, gain, wr, rbias)


def _routing_weights(logits):
    lane_i = lax.broadcasted_iota(jnp.int32, logits.shape, 1)
    lane = lane_i.astype(F32)
    neg_inf = jnp.float32(-jnp.inf)
    big = jnp.float32(1 << 20)
    is_group = lane_i < N_EXPERT_GROUPS
    gl = jnp.where(is_group, logits, neg_inf)
    gmax = jnp.max(gl, axis=-1, keepdims=True)
    gsum = jnp.sum(jnp.exp(gl - gmax), axis=-1, keepdims=True)
    p_g = 1.0 / gsum
    g_idx = jnp.min(jnp.where(gl == gmax, lane, big), axis=-1, keepdims=True)
    n_exp = N_EXPERT_GROUPS * EXPERTS_PER_GROUP
    is_exp = jnp.logical_and(lane_i >= N_EXPERT_GROUPS, lane_i < N_EXPERT_GROUPS + n_exp)
    exp_group = jnp.right_shift(lane_i - N_EXPERT_GROUPS, 2).astype(F32)
    assert EXPERTS_PER_GROUP == 4
    sel = jnp.logical_and(is_exp, exp_group == g_idx)
    el = jnp.where(sel, logits, neg_inf)
    emax = jnp.max(el, axis=-1, keepdims=True)
    ee = jnp.exp(el - emax)
    esum = jnp.sum(ee, axis=-1, keepdims=True)
    idx1 = jnp.min(jnp.where(el == emax, lane, big), axis=-1, keepdims=True)
    el2 = jnp.where(lane == idx1, neg_inf, el)
    e2max = jnp.max(el2, axis=-1, keepdims=True)
    idx2 = jnp.min(jnp.where(el2 == e2max, lane, big), axis=-1, keepdims=True)
    p1 = 1.0 / esum
    p2 = jnp.exp(e2max - emax) / esum
    top_sum = p1 + p2
    w = jnp.where(lane == idx1, p1 / top_sum, jnp.where(lane == idx2, p2 / top_sum, 0.0))
    return w * p_g, g_idx


def _moe_kernel(h2_ref, logit_ref, x1_ref, wg_ref, wu_ref, wd_ref, lstrict_ref, out_ref,
                hs_scr, cws_scr, pos_scr, acc_scr, off_smem, *, tm, sub):
    g = pl.program_id(1)
    lane = lax.broadcasted_iota(jnp.int32, (tm, LANES), 1)

    @pl.when(g == 0)
    def _():
        cw, g_idx = _routing_weights(logit_ref[...])
        onehot = jnp.where(lane.astype(F32) == g_idx, 1.0, 0.0)
        earlier = _dot(lstrict_ref[...], onehot.astype(BF16))
        counts = jnp.sum(onehot, axis=0, keepdims=True)
        lane_row = lax.broadcasted_iota(jnp.int32, (1, LANES), 1)
        start = jnp.zeros((1, LANES), F32)
        for k in range(N_EXPERT_GROUPS - 1):
            start = start + jnp.where(lane_row > k, counts[:, k:k + 1], 0.0)
        pos = jnp.sum(onehot * (earlier + start), axis=-1, keepdims=True)
        pos_b = jnp.broadcast_to(pos, (tm, LANES))
        pos_scr[...] = pos_b
        pos_row = jnp.transpose(pos_b)[0:1, :]
        dest = lax.broadcasted_iota(jnp.int32, (tm, tm), 0).astype(F32)
        gather = jnp.where(dest == pos_row, 1.0, 0.0).astype(BF16)
        hs_scr[...] = _dot(gather, h2_ref[...]).astype(BF16)
        cw_sorted = _dot(gather, jnp.concatenate(_split2(cw), axis=1))
        cws_scr[...] = cw_sorted[:, :LANES] + cw_sorted[:, LANES:]
        for k in range(N_EXPERT_GROUPS):
            off_smem[k] = start[0, k].astype(jnp.int32)
        off_smem[N_EXPERT_GROUPS] = jnp.int32(tm)
        acc_scr[...] = jnp.zeros_like(acc_scr)

    first = off_smem[g]
    last = off_smem[g + 1]
    lane_s = lax.broadcasted_iota(jnp.int32, (sub, LANES), 1)
    for s in range(tm // sub):
        @pl.when(jnp.logical_and(first < (s + 1) * sub, last > s * sub))
        def _(s=s):
            rows = slice(s * sub, (s + 1) * sub)
            hs = hs_scr[rows, :]
            hg = _dot(hs, wg_ref[...])
            hu = _dot(hs, wu_ref[...])
            cw = cws_scr[rows, :]
            parts = []
            for j in range(EXPERTS_PER_GROUP):
                cols = slice(j * D_EXPERT, (j + 1) * D_EXPERT)
                want = N_EXPERT_GROUPS + g * EXPERTS_PER_GROUP + j
                c_col = jnp.sum(jnp.where(lane_s == want, cw, 0.0), axis=-1, keepdims=True)
                parts.append((_silu(hg[:, cols]) * hu[:, cols] * c_col).astype(BF16))
            act = jnp.concatenate(parts, axis=-1)
            acc_scr[rows, :] += _dot(act, wd_ref[...])

    @pl.when(g == N_EXPERT_GROUPS - 1)
    def _():
        src = lax.broadcasted_iota(jnp.int32, (tm, tm), 1).astype(F32)
        scatter = jnp.where(pos_scr[:, 0:1] == src, 1.0, 0.0).astype(BF16)
        out_ref[...] = x1_ref[...] + _dot(scatter, acc_scr[...].astype(BF16))


def _moe(h2, logits, x1, wg, wu, wd, tm, sub):
    t, d = x1.shape
    gw = EXPERTS_PER_GROUP * D_EXPERT
    lstrict = jnp.tril(jnp.ones((tm, tm), F32), -1).astype(BF16)
    return pl.pallas_call(
        functools.partial(_moe_kernel, tm=tm, sub=sub),
        grid=(t // tm, N_EXPERT_GROUPS),
        in_specs=[
            pl.BlockSpec((tm, d), lambda i, g: (i, 0)),
            pl.BlockSpec((tm, LANES), lambda i, g: (i, 0)),
            pl.BlockSpec((tm, d), lambda i, g: (i, 0)),
            pl.BlockSpec((None, d, gw), lambda i, g: (g, 0, 0)),
            pl.BlockSpec((None, d, gw), lambda i, g: (g, 0, 0)),
            pl.BlockSpec((None, gw, d), lambda i, g: (g, 0, 0)),
            pl.BlockSpec((tm, tm), lambda i, g: (0, 0)),
        ],
        out_specs=pl.BlockSpec((tm, d), lambda i, g: (i, 0)),
        out_shape=jax.ShapeDtypeStruct((t, d), F32),
        scratch_shapes=[pltpu.VMEM((tm, d), BF16), pltpu.VMEM((tm, LANES), F32),
                        pltpu.VMEM((tm, LANES), F32), pltpu.VMEM((tm, d), F32),
                        pltpu.SMEM((8,), jnp.int32)],
        compiler_params=_params(("parallel", "arbitrary")),
        name="moe",
    )(h2, logits, x1, wg, wu, wd, lstrict)


def _ple_kernel(x_ref, p_ref, gain_ref, wpg_ref, wple_ref, out_ref):
    x = x_ref[...]
    h3 = x * lax.rsqrt(jnp.mean(x * x, axis=-1, keepdims=True) + EPS) * gain_ref[...]
    gate = _sigmoid(_dot(h3.astype(BF16), wpg_ref[...]))
    ple = _dot(p_ref[...].astype(BF16), wple_ref[...])
    out_ref[...] = x + gate * ple


def _ple(x2, p2, gain, wpg, wple, tm):
    t, d = x2.shape
    pd = p2.shape[1]
    return pl.pallas_call(
        _ple_kernel,
        grid=(t // tm,),
        in_specs=[
            pl.BlockSpec((tm, d), lambda i: (i, 0)),
            pl.BlockSpec((tm, pd), lambda i: (i, 0)),
            pl.BlockSpec((1, d), lambda i: (0, 0)),
            pl.BlockSpec((d, d), lambda i: (0, 0)),
            pl.BlockSpec((pd, d), lambda i: (0, 0)),
        ],
        out_specs=pl.BlockSpec((tm, d), lambda i: (i, 0)),
        out_shape=jax.ShapeDtypeStruct((t, d), F32),
        compiler_params=_params(("parallel",)),
        name="ple",
    )(x2, p2, gain, wpg, wple)


def _lane_pad(v, offset):
    out = jnp.zeros((1, LANES), F32)
    return out.at[0, offset:offset + v.shape[0]].set(v.astype(F32))


def _layer(x2, p2, b, s, norm_mix, w_in, q_norm, k_norm, conv_w, a_log, dt_bias, dn_out_norm,
           w_branch_a, w_branch_b, w_out, norm_ffn, w_router_group, b_router_group,
           w_router_expert, b_router_expert, w_expert_gate, w_expert_up, w_expert_down,
           norm_ple, w_ple, w_ple_gate):
    t, d = x2.shape
    tiles = _tiles(s)
    qa = N_GROUPS * GROUP_WIDTH
    o_b0 = 3 * qa
    o_z = o_b0 + 3 * DN_WIDTH
    o_beta = o_z + DN_WIDTH
    o_alpha = o_beta + DN_HEADS
    o_ga = o_alpha + DN_HEADS
    o_gb = o_ga + d
    w_in = w_in.astype(F32)
    gw_a = GROUP_WIDTH
    w_groups = [jnp.concatenate([w_in[:, part * qa + g * gw_a:part * qa + (g + 1) * gw_a]
                                 for part in range(3)], axis=1).astype(BF16) for g in range(N_GROUPS)]
    w_plain = jnp.concatenate([
        w_in[:, o_b0:o_z],
        w_in[:, o_z:o_beta],
        w_in[:, o_ga:o_gb],
        w_in[:, o_gb:o_gb + d],
    ], axis=1).astype(BF16)
    w_ab = jnp.zeros((d, LANES), F32).at[:, 0:2 * DN_HEADS].set(w_in[:, o_beta:o_ga])
    wab = jnp.concatenate(_split2(w_ab), axis=1)
    qg = jnp.tile(q_norm.astype(F32), HEADS_A) * (HEAD_DIM_A ** -0.5 * LOG2E)
    kg = jnp.tile(k_norm.astype(F32), HEADS_A)
    qkgain = jnp.concatenate([qg, kg])[None, :]

    proj, ab, qkv = _in_projection(x2, norm_mix.astype(F32)[None, :], w_plain, w_groups, wab,
                                   qkgain, b, s, tm=tiles["in_proj"])
    proj3 = proj.reshape(b, s, PROJ_WIDTH)

    (o0, l0), (o1, l1), (o2, l2) = [_attention_group(a, 0, 1, 2, tq=tiles["attn"]) for a in qkv]

    o_b = _deltanet(proj3, ab.reshape(b, s, LANES), conv_w.astype(F32),
                    _lane_pad(a_log, DN_HEADS), _lane_pad(dt_bias, DN_HEADS),
                    dn_out_norm.astype(F32)[None, :], tile=tiles["deltanet"]).reshape(t, DN_WIDTH)

    w_r = jnp.zeros((d, LANES), F32)
    w_r = w_r.at[:, 0:N_EXPERT_GROUPS].set(w_router_group.astype(F32))
    n_exp = N_EXPERT_GROUPS * EXPERTS_PER_GROUP
    w_r = w_r.at[:, N_EXPERT_GROUPS:N_EXPERT_GROUPS + n_exp].set(w_router_expert.astype(F32))
    w_r2 = jnp.concatenate(_split2(w_r), axis=1)
    rbias = _lane_pad(jnp.concatenate([b_router_group.astype(F32), b_router_expert.astype(F32)]), 0)

    x1, h2, logits = _merge(o0.reshape(t, GROUP_WIDTH), l0.reshape(t, LANES), (o1, o2), (l1, l2), o_b,
                            proj, x2, w_branch_a.astype(BF16), w_branch_b.astype(BF16),
                            w_out.astype(BF16), norm_ffn.astype(F32)[None, :], w_r2, rbias,
                            s, tm=tiles["merge"])

    gw = EXPERTS_PER_GROUP * D_EXPERT
    wg = jnp.transpose(w_expert_gate, (0, 2, 1, 3)).reshape(N_EXPERT_GROUPS, d, gw).astype(BF16)
    wu = jnp.transpose(w_expert_up, (0, 2, 1, 3)).reshape(N_EXPERT_GROUPS, d, gw).astype(BF16)
    wd = w_expert_down.reshape(N_EXPERT_GROUPS, gw, d).astype(BF16)
    x2n = _moe(h2, logits, x1, wg, wu, wd, tm=tiles["moe"], sub=tiles["moe_slab"])

    return _ple(x2n, p2, norm_ple.astype(F32)[None, :], w_ple_gate.astype(BF16), w_ple.astype(BF16),
                tm=tiles["ple"])


def kernel(x, p, norm_mix, w_in, q_norm, k_norm, conv_w, a_log, dt_bias, dn_out_norm, w_branch_a, w_branch_b, w_out, norm_ffn, w_router_group, b_router_group, w_router_expert, b_router_expert, w_expert_gate, w_expert_up, w_expert_down, norm_ple, w_ple, w_ple_gate):
    b, s, d = x.shape
    depth = w_in.shape[0]
    x2 = x.astype(F32).reshape(b * s, d)
    for i in range(depth):
        x2 = _layer(x2, p[i].reshape(b * s, -1), b, s, norm_mix[i], w_in[i], q_norm[i], k_norm[i],
                    conv_w[i], a_log[i], dt_bias[i], dn_out_norm[i], w_branch_a[i], w_branch_b[i],
                    w_out[i], norm_ffn[i], w_router_group[i], b_router_group[i], w_router_expert[i],
                    b_router_expert[i], w_expert_gate[i], w_expert_up[i], w_expert_down[i],
                    norm_ple[i], w_ple[i], w_ple_gate[i])
    return x2.reshape(b, s, d)
```

```python
import functools

import jax
import jax.numpy as jnp
from jax import lax
from jax.experimental import pallas as pl
from jax.experimental.pallas import tpu as pltpu

F32 = jnp.float32
BF16 = jnp.bfloat16
EPS = 1e-6
LN2 = 0.6931471805599453
LOG2E = 1.4426950408889634

ATTN_GROUPS = ((128, 1), (512, 4), (2048, 16))
N_GROUPS = 3
HEADS_A = 8
HEAD_DIM_A = 64
GROUP_WIDTH = HEADS_A * HEAD_DIM_A
ATTN_BLOCK = 128
DN_HEADS = 8
DN_DIM = 128
DN_WIDTH = DN_HEADS * DN_DIM
CONV_WIDTH = 4
CHUNK = 64
N_EXPERT_GROUPS = 4
EXPERTS_PER_GROUP = 4
D_EXPERT = 256
SUBLANES = 8
LANES = 128

PROJ_TILE = 1536
COL_B, COL_Z, COL_GA, COL_GB = 0, 3072, 4096, 5120
PROJ_WIDTH = 6144

V7X_VMEM_BYTES = 64 * 1024 * 1024
VMEM_LIMIT = V7X_VMEM_BYTES * 7 // 8


def _tiles(s):
    return dict(
        in_proj=min(1024, s),
        attn=512,
        deltanet=min(512, s),
        merge=min(512, s),
        moe=min(1024, s),
        moe_slab=128,
        ple=min(1024, s),
    )


def _dot(a, b):
    return jnp.dot(a, b, preferred_element_type=F32)


def _dot_nt(a, b):
    return lax.dot_general(a, b, (((1,), (1,)), ((), ())), preferred_element_type=F32)


def _dot_tn(a, b):
    return lax.dot_general(a, b, (((0,), (0,)), ((), ())), preferred_element_type=F32)


def _split2(v):
    hi = v.astype(BF16)
    lo = (v - hi.astype(F32)).astype(BF16)
    return hi, lo


def _split3(v):
    a = v.astype(BF16)
    r = v - a.astype(F32)
    b = r.astype(BF16)
    c = (r - b.astype(F32)).astype(BF16)
    return a, b, c


def _sigmoid(v):
    return 0.5 + 0.5 * jnp.tanh(0.5 * v)


def _silu(v):
    half = 0.5 * v
    return half + half * jnp.tanh(half)


def _heads_to_lanes(per_head):
    lane = lax.broadcasted_iota(jnp.int32, per_head.shape, 1)
    heads_per_slab = LANES // HEAD_DIM_A
    outs = []
    for c in range(GROUP_WIDTH // LANES):
        idx = lane // HEAD_DIM_A + heads_per_slab * c
        outs.append(jnp.take_along_axis(per_head, idx, axis=1))
    return jnp.concatenate(outs, axis=1)


def _params(sem):
    return pltpu.CompilerParams(dimension_semantics=sem, vmem_limit_bytes=VMEM_LIMIT)


def _inproj_main_kernel(x_ref, gain_ref, w_ref, wab_ref, proj_ref, ab_ref, h_ref):
    @pl.when(pl.program_id(1) == 0)
    def _():
        x = x_ref[...]
        ms = jnp.mean(x * x, axis=-1, keepdims=True)
        h = x * lax.rsqrt(ms + EPS) * gain_ref[...]
        hi, lo = _split2(h)
        h_ref[...] = hi
        hi_terms = _dot(hi, wab_ref[...])
        ab_ref[...] = hi_terms[:, :LANES] + hi_terms[:, LANES:] + _dot(lo, wab_ref[:, :LANES])

    proj_ref[...] = _dot(h_ref[...], w_ref[...]).astype(BF16)


def _inproj_attn_kernel(h_ref, w_ref, qkgain_ref, red_ref, out_ref, perm_scr, *, tm, g):
    slabs = GROUP_WIDTH // LANES

    def normed(a, gain):
        ss = _dot((a * a).astype(BF16), red_ref[...])
        r = lax.rsqrt(ss * (1.0 / HEAD_DIM_A) + EPS)
        return a * _heads_to_lanes(r) * gain

    h = h_ref[...]
    dil = ATTN_GROUPS[g][1]
    n_u = tm // dil
    for p in range(3):
        pcols = slice(p * GROUP_WIDTH, (p + 1) * GROUP_WIDTH)
        val = _dot(h, w_ref[:, pcols])
        if p < 2:
            val = normed(val, qkgain_ref[:, pcols])
        if dil == 1:
            out_ref[0, :, pcols] = val.astype(BF16)
        else:
            if dil < SUBLANES:
                for c in range(slabs):
                    perm_scr[c, 0:tm, :] = val[:, c * LANES:(c + 1) * LANES]
                for r in range(dil):
                    for c in range(slabs):
                        lo = p * GROUP_WIDTH + c * LANES
                        out_ref[r, :, lo:lo + LANES] = (
                            perm_scr[c, pl.ds(r, n_u, stride=dil), :].astype(BF16))
            else:
                pitch = n_u + 1
                for c in range(slabs):
                    for m in range(tm // SUBLANES):
                        u, r0 = divmod(m * SUBLANES, dil)
                        perm_scr[c, pl.ds(r0 * pitch + u, SUBLANES, stride=pitch), :] = (
                            val[m * SUBLANES:(m + 1) * SUBLANES, c * LANES:(c + 1) * LANES])
                for r in range(dil):
                    for c in range(slabs):
                        lo = p * GROUP_WIDTH + c * LANES
                        out_ref[r, :, lo:lo + LANES] = perm_scr[c, pl.ds(r * pitch, n_u), :].astype(BF16)


def _in_projection(x2, norm_gain, w_plain, w_groups, wab, qkgain, b, s, tm):
    t, d = x2.shape
    assert w_plain.shape[1] == PROJ_WIDTH and PROJ_WIDTH % PROJ_TILE == 0
    proj, ab, h = pl.pallas_call(
        _inproj_main_kernel,
        grid=(t // tm, PROJ_WIDTH // PROJ_TILE),
        in_specs=[
            pl.BlockSpec((tm, d), lambda i, j: (i, 0)),
            pl.BlockSpec((1, d), lambda i, j: (0, 0)),
            pl.BlockSpec((d, PROJ_TILE), lambda i, j: (0, j)),
            pl.BlockSpec((d, 2 * LANES), lambda i, j: (0, 0)),
        ],
        out_specs=[
            pl.BlockSpec((tm, PROJ_TILE), lambda i, j: (i, j)),
            pl.BlockSpec((tm, LANES), lambda i, j: (i, 0)),
            pl.BlockSpec((tm, d), lambda i, j: (i, 0)),
        ],
        out_shape=[
            jax.ShapeDtypeStruct((t, PROJ_WIDTH), BF16),
            jax.ShapeDtypeStruct((t, LANES), F32),
            jax.ShapeDtypeStruct((t, d), BF16),
        ],
        compiler_params=_params(("parallel", "arbitrary")),
        name="in_proj",
    )(x2, norm_gain, w_plain, wab)

    tps = s // tm
    col = jnp.arange(GROUP_WIDTH) // HEAD_DIM_A
    red = (col[:, None] == jnp.arange(LANES)[None, :]).astype(BF16)
    qkv = []
    for g, (_, dil) in enumerate(ATTN_GROUPS):
        qkv.append(pl.pallas_call(
            functools.partial(_inproj_attn_kernel, tm=tm, g=g),
            grid=(t // tm,),
            in_specs=[
                pl.BlockSpec((tm, d), lambda i: (i, 0)),
                pl.BlockSpec((d, PROJ_TILE), lambda i: (0, 0)),
                pl.BlockSpec((1, 2 * GROUP_WIDTH), lambda i: (0, 0)),
                pl.BlockSpec((GROUP_WIDTH, LANES), lambda i: (0, 0)),
            ],
            out_specs=pl.BlockSpec((None, dil, tm // dil, PROJ_TILE), lambda i: (i // tps, 0, i % tps, 0)),
            out_shape=jax.ShapeDtypeStruct((b, dil, s // dil, PROJ_TILE), BF16),
            scratch_shapes=[pltpu.VMEM((GROUP_WIDTH // LANES, tm + dil, LANES), F32)],
            compiler_params=_params(("parallel",)),
            name=f"in_proj_attn_d{dil}",
        )(h, w_groups[g], qkgain, red))
    return proj, ab, qkv


def _attn_kernel(q_ref, kc_ref, kp_ref, vc_ref, vp_ref, o_ref, lse_ref, *, n_blk, pairs_at_once=4):
    n = pl.program_id(2)
    blk = ATTN_BLOCK
    qi = lax.broadcasted_iota(jnp.int32, (2 * blk, blk), 0) & (blk - 1)
    kj = lax.broadcasted_iota(jnp.int32, (2 * blk, blk), 1)
    cur_ok2 = kj <= qi
    prev_band = kj >= qi
    lane = lax.broadcasted_iota(jnp.int32, (blk, LANES), 1)
    low_half = lane < HEAD_DIM_A
    neg_inf = jnp.float32(-jnp.inf)
    ones = jnp.ones((blk, LANES), BF16)
    zero = jnp.zeros((blk, LANES), BF16)

    for b in range(n_blk):
        rows = slice(b * blk, (b + 1) * blk)
        if b == 0:
            prev_ok2 = jnp.logical_and(prev_band, n > 0)
        else:
            prev_ok2 = prev_band
        lse_tile = jnp.zeros((blk, LANES), F32)
        for p0 in range(0, HEADS_A // 2, pairs_at_once):
            pairs = range(p0, p0 + pairs_at_once)
            cols = [slice(pr * LANES, (pr + 1) * LANES) for pr in pairs]
            q2 = [q_ref[rows, c] for c in cols]
            kc2 = [kc_ref[rows, c] for c in cols]
            vc2 = [vc_ref[rows, c] for c in cols]
            if b == 0:
                kp2 = [kp_ref[:, c] for c in cols]
                vp2 = [vp_ref[:, c] for c in cols]
            else:
                prow = slice((b - 1) * blk, b * blk)
                kp2 = [kc_ref[prow, c] for c in cols]
                vp2 = [vc_ref[prow, c] for c in cols]
            q_st = [jnp.concatenate([jnp.where(low_half, q, zero), jnp.where(low_half, zero, q)], axis=0)
                    for q in q2]
            vc_aug = [jnp.concatenate([x, ones], axis=1) for x in vc2]
            vp_aug = [jnp.concatenate([x, ones], axis=1) for x in vp2]
            s_c = [jnp.where(cur_ok2, _dot_nt(q, kk), neg_inf) for q, kk in zip(q_st, kc2)]
            s_p = [jnp.where(prev_ok2, _dot_nt(q, kk), neg_inf) for q, kk in zip(q_st, kp2)]
            m = [jnp.maximum(jnp.max(sc, axis=-1, keepdims=True), jnp.max(sp, axis=-1, keepdims=True))
                 for sc, sp in zip(s_c, s_p)]
            e_c = [jnp.exp2(sc - mh).astype(BF16) for sc, mh in zip(s_c, m)]
            e_p = [jnp.exp2(sp - mh).astype(BF16) for sp, mh in zip(s_p, m)]
            nd = [_dot(ec, va) + _dot(ep, vb) for ec, va, ep, vb in zip(e_c, vc_aug, e_p, vp_aug)]
            outs = [x[:, :LANES] / x[:, LANES:] for x in nd]
            lse = [mh * LN2 + jnp.log(x[:, LANES:LANES + 1]) for mh, x in zip(m, nd)]
            for i, pr in enumerate(pairs):
                for half in range(2):
                    hrows = slice(half * blk, (half + 1) * blk)
                    lse_tile = jnp.where(lane == 2 * pr + half, lse[i][hrows], lse_tile)
                o_ref[rows, cols[i]] = jnp.where(low_half, outs[i][0:blk], outs[i][blk:2 * blk]).astype(BF16)
        lse_ref[rows, :] = lse_tile


def _attention_group(arr4, cq, ck, cv, tq):
    b, dil, n_sub, _ = arr4.shape
    tq = min(tq, n_sub)
    n_blk = tq // ATTN_BLOCK

    def cur(c0):
        return pl.BlockSpec((None, None, tq, GROUP_WIDTH), lambda bi, r, n: (bi, r, n, c0))

    def prev(c0):
        return pl.BlockSpec((None, None, ATTN_BLOCK, GROUP_WIDTH),
                            lambda bi, r, n: (bi, r, jnp.maximum(n * n_blk - 1, 0), c0))

    return pl.pallas_call(
        functools.partial(_attn_kernel, n_blk=n_blk),
        grid=(b, dil, n_sub // tq),
        in_specs=[cur(cq), cur(ck), prev(ck), cur(cv), prev(cv)],
        out_specs=[
            pl.BlockSpec((None, None, tq, GROUP_WIDTH), lambda bi, r, n: (bi, r, n, 0)),
            pl.BlockSpec((None, None, tq, LANES), lambda bi, r, n: (bi, r, n, 0)),
        ],
        out_shape=[
            jax.ShapeDtypeStruct((b, dil, n_sub, GROUP_WIDTH), BF16),
            jax.ShapeDtypeStruct((b, dil, n_sub, LANES), F32),
        ],
        compiler_params=_params(("parallel", "parallel", "parallel")),
        name=f"attn_d{dil}",
    )(arr4, arr4, arr4, arr4, arr4)


def _deltanet_kernel(xb_ref, z_ref, ab_ref, convw_ref, alog_ref, dtb_ref, ognorm_ref, lcum_ref,
                     o_ref, qkv_scr, gc_scr, beta_scr, carry_scr, state_scr, ext_scr,
                     u_scr, w_scr, qd_scr, kt_scr, attn_scr, *, tile, chunks_per_iter=8):
    t_idx = pl.program_id(1)
    n_chunks = tile // CHUNK
    halo = 8
    heads = range(DN_HEADS)

    @pl.when(t_idx == 0)
    def _():
        carry_scr[...] = jnp.zeros_like(carry_scr)
        state_scr[...] = jnp.zeros_like(state_scr)

    for c in range(3 * DN_HEADS):
        cols = slice(c * DN_DIM, (c + 1) * DN_DIM)
        cur = xb_ref[:, cols].astype(F32)
        ext_scr[c, 0:halo, :] = carry_scr[:, cols]
        ext_scr[c, halo:halo + tile, :] = cur
        y = cur * convw_ref[CONV_WIDTH - 1:CONV_WIDTH, cols]
        for j in range(CONV_WIDTH - 1):
            off = halo - (CONV_WIDTH - 1) + j
            y = y + ext_scr[c, pl.ds(off, tile), :] * convw_ref[j:j + 1, cols]
        carry_scr[:, cols] = cur[tile - halo:tile]
        y = _silu(y)
        if c < 2 * DN_HEADS:
            inv = lax.rsqrt(jnp.sum(y * y, axis=-1, keepdims=True) + EPS)
            if c < DN_HEADS:
                inv = inv * (DN_DIM ** -0.5)
            y = y * inv
        qkv_scr[:, cols] = y

    ab = ab_ref[...]
    beta_scr[...] = _sigmoid(ab)
    sp_in = ab + dtb_ref[...]
    softplus = jnp.maximum(sp_in, 0.0) + jnp.log(1.0 + jnp.exp(-jnp.abs(sp_in)))
    gdec = -jnp.exp(alog_ref[...]) * softplus
    g1, g2, g3 = _split3(gdec)
    lcum = lcum_ref[...]
    gc_scr[...] = _dot(lcum, g1) + _dot(lcum, g2) + _dot(lcum, g3)

    ri = lax.broadcasted_iota(jnp.int32, (CHUNK, CHUNK), 0)
    ci = lax.broadcasted_iota(jnp.int32, (CHUNK, CHUNK), 1)
    tril = ci <= ri
    tril_strict = ci < ri
    eye = (ci == ri).astype(F32)
    neg_inf = jnp.float32(-jnp.inf)

    def intra_body(c, carry):
        rows_c = [pl.ds(pl.multiple_of((c * chunks_per_iter + i) * CHUNK, CHUNK), CHUNK)
                  for i in range(chunks_per_iter)]
        units = [(i, h) for i in range(chunks_per_iter) for h in heads]
        gc_c = [gc_scr[r, :] for r in rows_c]
        gc_t_c = [jnp.transpose(g) for g in gc_c]
        beta_c = [beta_scr[r, :] for r in rows_c]
        q = [qkv_scr[rows_c[i], h * DN_DIM:(h + 1) * DN_DIM] for i, h in units]
        k = [qkv_scr[rows_c[i], DN_WIDTH + h * DN_DIM:DN_WIDTH + (h + 1) * DN_DIM] for i, h in units]
        v = [qkv_scr[rows_c[i], 2 * DN_WIDTH + h * DN_DIM:2 * DN_WIDTH + (h + 1) * DN_DIM] for i, h in units]

        b_col = [beta_c[i][:, h:h + 1] for i, h in units]
        g_col = [gc_c[i][:, DN_HEADS + h:DN_HEADS + h + 1] for i, h in units]
        g_row = [gc_t_c[i][DN_HEADS + h:DN_HEADS + h + 1, :] for i, h in units]
        g_last = [gcol[CHUNK - 1:CHUNK, :] for gcol in g_col]
        decay = [jnp.exp(jnp.where(tril, gcol - grow, neg_inf)) for gcol, grow in zip(g_col, g_row)]
        e_g = [jnp.exp(gcol) for gcol in g_col]
        kb = [kh * bh for kh, bh in zip(k, b_col)]
        k16 = [kh.astype(BF16) for kh in k]
        a_mat = [jnp.where(tril_strict, _dot_nt(kbh.astype(BF16), kh16) * dh, 0.0)
                 for kbh, kh16, dh in zip(kb, k16, decay)]
        pw = [-a for a in a_mat]
        t_mat = [eye + p for p in pw]
        for _ in range(5):
            pw16 = [p.astype(BF16) for p in pw]
            pw = [_dot(p, p) for p in pw16]
            t_mat = [th + _dot(th.astype(BF16), p.astype(BF16)) for th, p in zip(t_mat, pw)]
        t16 = [th.astype(BF16) for th in t_mat]
        u = [_dot(th, (vh * bh).astype(BF16)) for th, vh, bh in zip(t16, v, b_col)]
        w = [_dot(th, (kbh * eh).astype(BF16)) for th, kbh, eh in zip(t16, kb, e_g)]
        attn = [jnp.where(tril, _dot_nt(qh.astype(BF16), kh16) * dh, 0.0).astype(BF16)
                for qh, kh16, dh in zip(q, k16, decay)]
        q_dec = [(qh * eh).astype(BF16) for qh, eh in zip(q, e_g)]
        k_tail = [(kh * jnp.exp(gl - gcol)).astype(BF16) for kh, gl, gcol in zip(k, g_last, g_col)]
        for n, (i, h) in enumerate(units):
            hc = slice(h * DN_DIM, (h + 1) * DN_DIM)
            u_scr[rows_c[i], hc] = u[n]
            w_scr[rows_c[i], hc] = w[n].astype(BF16)
            qd_scr[rows_c[i], hc] = q_dec[n]
            kt_scr[rows_c[i], hc] = k_tail[n]
            attn_scr[h, rows_c[i], :] = attn[n]
        return carry

    lax.fori_loop(0, n_chunks // chunks_per_iter, intra_body, 0)

    def scan_body(c, carry):
        r0 = pl.multiple_of(c * CHUNK, CHUNK)
        rows = pl.ds(r0, CHUNK)
        decay_last = jnp.exp(gc_scr[pl.ds(r0 + CHUNK - 1, 1), :])
        hcs = [slice(h * DN_DIM, (h + 1) * DN_DIM) for h in heads]
        state = [state_scr[h] for h in heads]
        u = [u_scr[rows, hc] for hc in hcs]
        w16 = [w_scr[rows, hc] for hc in hcs]
        qd = [qd_scr[rows, hc] for hc in hcs]
        kt = [kt_scr[rows, hc] for hc in hcs]
        attn = [attn_scr[h, rows, :] for h in heads]
        z = [z_ref[rows, hc] for hc in hcs]
        s16 = [sh.astype(BF16) for sh in state]
        v_new = [(uh - _dot(wh, sh)).astype(BF16) for uh, wh, sh in zip(u, w16, s16)]
        o = [_dot(qh, sh) + _dot(ah, vn) for qh, sh, ah, vn in zip(qd, s16, attn, v_new)]
        new_state = [sh * decay_last[:, DN_HEADS + h:DN_HEADS + h + 1] + _dot_tn(kth, vn)
                     for h, sh, kth, vn in zip(heads, state, kt, v_new)]
        o = [oh * lax.rsqrt(jnp.mean(oh * oh, axis=-1, keepdims=True) + EPS) * ognorm_ref[...] for oh in o]
        o = [(oh * _silu(zh.astype(F32))).astype(BF16) for oh, zh in zip(o, z)]
        for h in heads:
            state_scr[h] = new_state[h]
        for h in heads:
            o_ref[rows, hcs[h]] = o[h]
        return carry

    lax.fori_loop(0, n_chunks, scan_body, 0, unroll=4)


def _deltanet(proj3, ab3, conv_w, a_log_l, dtb_l, og_norm, tile):
    b, s, w = proj3.shape
    lcum = jnp.tril(jnp.ones((tile, tile), F32))
    same_chunk = (jnp.arange(tile)[:, None] // CHUNK) == (jnp.arange(tile)[None, :] // CHUNK)
    lcum = jnp.where(same_chunk, lcum, 0.0).astype(BF16)
    return pl.pallas_call(
        functools.partial(_deltanet_kernel, tile=tile),
        grid=(b, s // tile),
        in_specs=[
            pl.BlockSpec((None, tile, 3 * DN_WIDTH), lambda bi, t: (bi, t, COL_B // (3 * DN_WIDTH))),
            pl.BlockSpec((None, tile, DN_WIDTH), lambda bi, t: (bi, t, COL_Z // DN_WIDTH)),
            pl.BlockSpec((None, tile, LANES), lambda bi, t: (bi, t, 0)),
            pl.BlockSpec((CONV_WIDTH, 3 * DN_WIDTH), lambda bi, t: (0, 0)),
            pl.BlockSpec((1, LANES), lambda bi, t: (0, 0)),
            pl.BlockSpec((1, LANES), lambda bi, t: (0, 0)),
            pl.BlockSpec((1, DN_DIM), lambda bi, t: (0, 0)),
            pl.BlockSpec((tile, tile), lambda bi, t: (0, 0)),
        ],
        out_specs=pl.BlockSpec((None, tile, DN_WIDTH), lambda bi, t: (bi, t, 0)),
        out_shape=jax.ShapeDtypeStruct((b, s, DN_WIDTH), BF16),
        scratch_shapes=[
            pltpu.VMEM((tile, 3 * DN_WIDTH), F32),
            pltpu.VMEM((tile, LANES), F32),
            pltpu.VMEM((tile, LANES), F32),
            pltpu.VMEM((8, 3 * DN_WIDTH), F32),
            pltpu.VMEM((DN_HEADS, DN_DIM, DN_DIM), F32),
            pltpu.VMEM((3 * DN_HEADS, 8 + tile, DN_DIM), F32),
            pltpu.VMEM((tile, DN_WIDTH), F32),
            pltpu.VMEM((tile, DN_WIDTH), BF16),
            pltpu.VMEM((tile, DN_WIDTH), BF16),
            pltpu.VMEM((tile, DN_WIDTH), BF16),
            pltpu.VMEM((DN_HEADS, tile, CHUNK), BF16),
        ],
        compiler_params=_params(("parallel", "arbitrary")),
        name="deltanet",
    )(proj3, proj3, ab3, conv_w, a_log_l, dtb_l, og_norm, lcum)


def _merge_kernel(o0_ref, o1_ref, o2_ref, l0_ref, l1_ref, l2_ref, ob_ref, ga_ref, gb_ref, x_ref,
                  wa_ref, wb_ref, wo_ref, hexp_ref, gain_ref, wr_ref, rbias_ref,
                  x1_ref, h2_ref, logit_ref, o1_scr, o2_scr, l1_scr, l2_scr, *, tm):
    for g, src, dst in ((1, o1_ref, o1_scr), (2, o2_ref, o2_scr), (1, l1_ref, l1_scr), (2, l2_ref, l2_scr)):
        dil = ATTN_GROUPS[g][1]
        for r in range(dil):
            for c in range(dst.shape[0]):
                dst[c, pl.ds(r, tm // dil, stride=dil), :] = src[r, :, c * LANES:(c + 1) * LANES].astype(F32)

    def slabs(scr):
        return jnp.concatenate([scr[c] for c in range(scr.shape[0])], axis=-1)

    l0, l1, l2 = l0_ref[...], l1_scr[0], l2_scr[0]
    m = jnp.maximum(jnp.maximum(l0, l1), l2)
    e0, e1, e2 = jnp.exp(l0 - m), jnp.exp(l1 - m), jnp.exp(l2 - m)
    tot = e0 + e1 + e2
    hexp = hexp_ref[...]
    o_a = jnp.zeros(o0_ref.shape, F32)
    for e, o_val in ((e0, o0_ref[...].astype(F32)), (e1, slabs(o1_scr)), (e2, slabs(o2_scr))):
        w_hi, w_lo = _split2(e / tot)
        o_a = o_a + _dot(jnp.concatenate([w_hi, w_lo], axis=1), hexp) * o_val
    ya = _dot(o_a.astype(BF16), wa_ref[...])
    yb = _dot(ob_ref[...], wb_ref[...])
    merged = _sigmoid(ga_ref[...].astype(F32)) * ya + _sigmoid(gb_ref[...].astype(F32)) * yb
    x1 = x_ref[...] + _dot(merged.astype(BF16), wo_ref[...])
    x1_ref[...] = x1
    h2 = x1 * lax.rsqrt(jnp.mean(x1 * x1, axis=-1, keepdims=True) + EPS) * gain_ref[...]
    hi, lo = _split2(h2)
    h2_ref[...] = hi
    hi_terms = _dot(hi, wr_ref[...])
    logit_ref[...] = (hi_terms[:, :LANES] + hi_terms[:, LANES:] + _dot(lo, wr_ref[:, :LANES])
                      + rbias_ref[...])


def _merge(o0, l0, o_perm, l_perm, o_b, proj, x2, wa, wb, wo, gain, wr, rbias, s, tm):
    t, d = x2.shape
    tps = s // tm
    head_of_col = jnp.arange(GROUP_WIDTH) // HEAD_DIM_A
    hexp = (jnp.arange(LANES)[:, None] == head_of_col[None, :]).astype(BF16)
    hexp = jnp.concatenate([hexp, hexp], axis=0)

    def row(wd, cblk=0):
        return pl.BlockSpec((tm, wd), lambda i: (i, cblk))

    def perm(arr):
        _, dil, _, wd = arr.shape
        return pl.BlockSpec((None, dil, tm // dil, wd), lambda i: (i // tps, 0, i % tps, 0))

    def full(a):
        return pl.BlockSpec(a.shape, lambda i: (0, 0))

    return pl.pallas_call(
        functools.partial(_merge_kernel, tm=tm),
        grid=(t // tm,),
        in_specs=[row(GROUP_WIDTH), perm(o_perm[0]), perm(o_perm[1]),
                  row(LANES), perm(l_perm[0]), perm(l_perm[1]),
                  row(DN_WIDTH), row(d, COL_GA // d), row(d, COL_GB // d), row(d),
                  full(wa), full(wb), full(wo), full(hexp), full(gain), full(wr),
                  full(rbias)],
        out_specs=[row(d), row(d), row(LANES)],
        out_shape=[
            jax.ShapeDtypeStruct((t, d), F32),
            jax.ShapeDtypeStruct((t, d), BF16),
            jax.ShapeDtypeStruct((t, LANES), F32),
        ],
        scratch_shapes=[pltpu.VMEM((GROUP_WIDTH // LANES, tm, LANES), F32),
                        pltpu.VMEM((GROUP_WIDTH // LANES, tm, LANES), F32),
                        pltpu.VMEM((1, tm, LANES), F32), pltpu.VMEM((1, tm, LANES), F32)],
        compiler_params=_params(("parallel",)),
        name="merge",
    )(o0, o_perm[0], o_perm[1], l0, l_perm[0], l_perm[1], o_b, proj, proj, x2,
      wa, wb, wo, hexp, gain, wr, rbias)


def _routing_weights(logits):
    lane_i = lax.broadcasted_iota(jnp.int32, logits.shape, 1)
    lane = lane_i.astype(F32)
    neg_inf = jnp.float32(-jnp.inf)
    big = jnp.float32(1 << 20)
    is_group = lane_i < N_EXPERT_GROUPS
    gl = jnp.where(is_group, logits, neg_inf)
    gmax = jnp.max(gl, axis=-1, keepdims=True)
    gsum = jnp.sum(jnp.exp(gl - gmax), axis=-1, keepdims=True)
    p_g = 1.0 / gsum
    g_idx = jnp.min(jnp.where(gl == gmax, lane, big), axis=-1, keepdims=True)
    n_exp = N_EXPERT_GROUPS * EXPERTS_PER_GROUP
    is_exp = jnp.logical_and(lane_i >= N_EXPERT_GROUPS, lane_i < N_EXPERT_GROUPS + n_exp)
    exp_group = jnp.right_shift(lane_i - N_EXPERT_GROUPS, 2).astype(F32)
    assert EXPERTS_PER_GROUP == 4
    sel = jnp.logical_and(is_exp, exp_group == g_idx)
    el = jnp.where(sel, logits, neg_inf)
    emax = jnp.max(el, axis=-1, keepdims=True)
    ee = jnp.exp(el - emax)
    esum = jnp.sum(ee, axis=-1, keepdims=True)
    idx1 = jnp.min(jnp.where(el == emax, lane, big), axis=-1, keepdims=True)
    el2 = jnp.where(lane == idx1, neg_inf, el)
    e2max = jnp.max(el2, axis=-1, keepdims=True)
    idx2 = jnp.min(jnp.where(el2 == e2max, lane, big), axis=-1, keepdims=True)
    p1 = 1.0 / esum
    p2 = jnp.exp(e2max - emax) / esum
    top_sum = p1 + p2
    w = jnp.where(lane == idx1, p1 / top_sum, jnp.where(lane == idx2, p2 / top_sum, 0.0))
    return w * p_g, g_idx


def _moe_kernel(h2_ref, logit_ref, x1_ref, wg_ref, wu_ref, wd_ref, lstrict_ref, out_ref,
                hs_scr, cws_scr, pos_scr, acc_scr, off_smem, *, tm, sub):
    g = pl.program_id(1)
    lane = lax.broadcasted_iota(jnp.int32, (tm, LANES), 1)

    @pl.when(g == 0)
    def _():
        cw, g_idx = _routing_weights(logit_ref[...])
        onehot = jnp.where(lane.astype(F32) == g_idx, 1.0, 0.0)
        earlier = _dot(lstrict_ref[...], onehot.astype(BF16))
        counts = jnp.sum(onehot, axis=0, keepdims=True)
        lane_row = lax.broadcasted_iota(jnp.int32, (1, LANES), 1)
        start = jnp.zeros((1, LANES), F32)
        for k in range(N_EXPERT_GROUPS - 1):
            start = start + jnp.where(lane_row > k, counts[:, k:k + 1], 0.0)
        pos = jnp.sum(onehot * (earlier + start), axis=-1, keepdims=True)
        pos_b = jnp.broadcast_to(pos, (tm, LANES))
        pos_scr[...] = pos_b
        pos_row = jnp.transpose(pos_b)[0:1, :]
        dest = lax.broadcasted_iota(jnp.int32, (tm, tm), 0).astype(F32)
        gather = jnp.where(dest == pos_row, 1.0, 0.0).astype(BF16)
        hs_scr[...] = _dot(gather, h2_ref[...]).astype(BF16)
        cw_sorted = _dot(gather, jnp.concatenate(_split2(cw), axis=1))
        cws_scr[...] = cw_sorted[:, :LANES] + cw_sorted[:, LANES:]
        for k in range(N_EXPERT_GROUPS):
            off_smem[k] = start[0, k].astype(jnp.int32)
        off_smem[N_EXPERT_GROUPS] = jnp.int32(tm)
        acc_scr[...] = jnp.zeros_like(acc_scr)

    first = off_smem[g]
    last = off_smem[g + 1]
    lane_s = lax.broadcasted_iota(jnp.int32, (sub, LANES), 1)
    for s in range(tm // sub):
        @pl.when(jnp.logical_and(first < (s + 1) * sub, last > s * sub))
        def _(s=s):
            rows = slice(s * sub, (s + 1) * sub)
            hs = hs_scr[rows, :]
            hg = _dot(hs, wg_ref[...])
            hu = _dot(hs, wu_ref[...])
            cw = cws_scr[rows, :]
            parts = []
            for j in range(EXPERTS_PER_GROUP):
                cols = slice(j * D_EXPERT, (j + 1) * D_EXPERT)
                want = N_EXPERT_GROUPS + g * EXPERTS_PER_GROUP + j
                c_col = jnp.sum(jnp.where(lane_s == want, cw, 0.0), axis=-1, keepdims=True)
                parts.append((_silu(hg[:, cols]) * hu[:, cols] * c_col).astype(BF16))
            act = jnp.concatenate(parts, axis=-1)
            acc_scr[rows, :] += _dot(act, wd_ref[...])

    @pl.when(g == N_EXPERT_GROUPS - 1)
    def _():
        src = lax.broadcasted_iota(jnp.int32, (tm, tm), 1).astype(F32)
        scatter = jnp.where(pos_scr[:, 0:1] == src, 1.0, 0.0).astype(BF16)
        out_ref[...] = x1_ref[...] + _dot(scatter, acc_scr[...].astype(BF16))


def _moe(h2, logits, x1, wg, wu, wd, tm, sub):
    t, d = x1.shape
    gw = EXPERTS_PER_GROUP * D_EXPERT
    lstrict = jnp.tril(jnp.ones((tm, tm), F32), -1).astype(BF16)
    return pl.pallas_call(
        functools.partial(_moe_kernel, tm=tm, sub=sub),
        grid=(t // tm, N_EXPERT_GROUPS),
        in_specs=[
            pl.BlockSpec((tm, d), lambda i, g: (i, 0)),
            pl.BlockSpec((tm, LANES), lambda i, g: (i, 0)),
            pl.BlockSpec((tm, d), lambda i, g: (i, 0)),
            pl.BlockSpec((None, d, gw), lambda i, g: (g, 0, 0)),
            pl.BlockSpec((None, d, gw), lambda i, g: (g, 0, 0)),
            pl.BlockSpec((None, gw, d), lambda i, g: (g, 0, 0)),
            pl.BlockSpec((tm, tm), lambda i, g: (0, 0)),
        ],
        out_specs=pl.BlockSpec((tm, d), lambda i, g: (i, 0)),
        out_shape=jax.ShapeDtypeStruct((t, d), F32),
        scratch_shapes=[pltpu.VMEM((tm, d), BF16), pltpu.VMEM((tm, LANES), F32),
                        pltpu.VMEM((tm, LANES), F32), pltpu.VMEM((tm, d), F32),
                        pltpu.SMEM((8,), jnp.int32)],
        compiler_params=_params(("parallel", "arbitrary")),
        name="moe",
    )(h2, logits, x1, wg, wu, wd, lstrict)


def _ple_kernel(x_ref, p_ref, gain_ref, wpg_ref, wple_ref, out_ref):
    x = x_ref[...]
    h3 = x * lax.rsqrt(jnp.mean(x * x, axis=-1, keepdims=True) + EPS) * gain_ref[...]
    gate = _sigmoid(_dot(h3.astype(BF16), wpg_ref[...]))
    ple = _dot(p_ref[...].astype(BF16), wple_ref[...])
    out_ref[...] = x + gate * ple


def _ple(x2, p2, gain, wpg, wple, tm):
    t, d = x2.shape
    pd = p2.shape[1]
    return pl.pallas_call(
        _ple_kernel,
        grid=(t // tm,),
        in_specs=[
            pl.BlockSpec((tm, d), lambda i: (i, 0)),
            pl.BlockSpec((tm, pd), lambda i: (i, 0)),
            pl.BlockSpec((1, d), lambda i: (0, 0)),
            pl.BlockSpec((d, d), lambda i: (0, 0)),
            pl.BlockSpec((pd, d), lambda i: (0, 0)),
        ],
        out_specs=pl.BlockSpec((tm, d), lambda i: (i, 0)),
        out_shape=jax.ShapeDtypeStruct((t, d), F32),
        compiler_params=_params(("parallel",)),
        name="ple",
    )(x2, p2, gain, wpg, wple)


def _lane_pad(v, offset):
    out = jnp.zeros((1, LANES), F32)
    return out.at[0, offset:offset + v.shape[0]].set(v.astype(F32))


def _layer(x2, p2, b, s, norm_mix, w_in, q_norm, k_norm, conv_w, a_log, dt_bias, dn_out_norm,
           w_branch_a, w_branch_b, w_out, norm_ffn, w_router_group, b_router_group,
           w_router_expert, b_router_expert, w_expert_gate, w_expert_up, w_expert_down,
           norm_ple, w_ple, w_ple_gate):
    t, d = x2.shape
    tiles = _tiles(s)
    qa = N_GROUPS * GROUP_WIDTH
    o_b0 = 3 * qa
    o_z = o_b0 + 3 * DN_WIDTH
    o_beta = o_z + DN_WIDTH
    o_alpha = o_beta + DN_HEADS
    o_ga = o_alpha + DN_HEADS
    o_gb = o_ga + d
    w_in = w_in.astype(F32)
    gw_a = GROUP_WIDTH
    w_groups = [jnp.concatenate([w_in[:, part * qa + g * gw_a:part * qa + (g + 1) * gw_a]
                                 for part in range(3)], axis=1).astype(BF16) for g in range(N_GROUPS)]
    w_plain = jnp.concatenate([
        w_in[:, o_b0:o_z],
        w_in[:, o_z:o_beta],
        w_in[:, o_ga:o_gb],
        w_in[:, o_gb:o_gb + d],
    ], axis=1).astype(BF16)
    w_ab = jnp.zeros((d, LANES), F32).at[:, 0:2 * DN_HEADS].set(w_in[:, o_beta:o_ga])
    wab = jnp.concatenate(_split2(w_ab), axis=1)
    qg = jnp.tile(q_norm.astype(F32), HEADS_A) * (HEAD_DIM_A ** -0.5 * LOG2E)
    kg = jnp.tile(k_norm.astype(F32), HEADS_A)
    qkgain = jnp.concatenate([qg, kg])[None, :]

    proj, ab, qkv = _in_projection(x2, norm_mix.astype(F32)[None, :], w_plain, w_groups, wab,
                                   qkgain, b, s, tm=tiles["in_proj"])
    proj3 = proj.reshape(b, s, PROJ_WIDTH)

    (o0, l0), (o1, l1), (o2, l2) = [_attention_group(a, 0, 1, 2, tq=tiles["attn"]) for a in qkv]

    o_b = _deltanet(proj3, ab.reshape(b, s, LANES), conv_w.astype(F32),
                    _lane_pad(a_log, DN_HEADS), _lane_pad(dt_bias, DN_HEADS),
                    dn_out_norm.astype(F32)[None, :], tile=tiles["deltanet"]).reshape(t, DN_WIDTH)

    w_r = jnp.zeros((d, LANES), F32)
    w_r = w_r.at[:, 0:N_EXPERT_GROUPS].set(w_router_group.astype(F32))
    n_exp = N_EXPERT_GROUPS * EXPERTS_PER_GROUP
    w_r = w_r.at[:, N_EXPERT_GROUPS:N_EXPERT_GROUPS + n_exp].set(w_router_expert.astype(F32))
    w_r2 = jnp.concatenate(_split2(w_r), axis=1)
    rbias = _lane_pad(jnp.concatenate([b_router_group.astype(F32), b_router_expert.astype(F32)]), 0)

    x1, h2, logits = _merge(o0.reshape(t, GROUP_WIDTH), l0.reshape(t, LANES), (o1, o2), (l1, l2), o_b,
                            proj, x2, w_branch_a.astype(BF16), w_branch_b.astype(BF16),
                            w_out.astype(BF16), norm_ffn.astype(F32)[None, :], w_r2, rbias,
                            s, tm=tiles["merge"])

    gw = EXPERTS_PER_GROUP * D_EXPERT
    wg = jnp.transpose(w_expert_gate, (0, 2, 1, 3)).reshape(N_EXPERT_GROUPS, d, gw).astype(BF16)
    wu = jnp.transpose(w_expert_up, (0, 2, 1, 3)).reshape(N_EXPERT_GROUPS, d, gw).astype(BF16)
    wd = w_expert_down.reshape(N_EXPERT_GROUPS, gw, d).astype(BF16)
    x2n = _moe(h2, logits, x1, wg, wu, wd, tm=tiles["moe"], sub=tiles["moe_slab"])

    return _ple(x2n, p2, norm_ple.astype(F32)[None, :], w_ple_gate.astype(BF16), w_ple.astype(BF16),
                tm=tiles["ple"])


def kernel(x, p, norm_mix, w_in, q_norm, k_norm, conv_w, a_log, dt_bias, dn_out_norm, w_branch_a, w_branch_b, w_out, norm_ffn, w_router_group, b_router_group, w_router_expert, b_router_expert, w_expert_gate, w_expert_up, w_expert_down, norm_ple, w_ple, w_ple_gate):
    b, s, d = x.shape
    depth = w_in.shape[0]
    x2 = x.astype(F32).reshape(b * s, d)
    for i in range(depth):
        x2 = _layer(x2, p[i].reshape(b * s, -1), b, s, norm_mix[i], w_in[i], q_norm[i], k_norm[i],
                    conv_w[i], a_log[i], dt_bias[i], dn_out_norm[i], w_branch_a[i], w_branch_b[i],
                    w_out[i], norm_ffn[i], w_router_group[i], b_router_group[i], w_router_expert[i],
                    b_router_expert[i], w_expert_gate[i], w_expert_up[i], w_expert_down[i],
                    norm_ple[i], w_ple[i], w_ple_gate[i])
    return x2.reshape(b, s, d)
```

```python
import functools

import jax
import jax.numpy as jnp
from jax import lax
from jax.experimental import pallas as pl
from jax.experimental.pallas import tpu as pltpu

F32 = jnp.float32
BF16 = jnp.bfloat16
EPS = 1e-6
LN2 = 0.6931471805599453
LOG2E = 1.4426950408889634

ATTN_GROUPS = ((128, 1), (512, 4), (2048, 16))
N_GROUPS = 3
HEADS_A = 8
HEAD_DIM_A = 64
GROUP_WIDTH = HEADS_A * HEAD_DIM_A
ATTN_BLOCK = 128
DN_HEADS = 8
DN_DIM = 128
DN_WIDTH = DN_HEADS * DN_DIM
CONV_WIDTH = 4
CHUNK = 64
N_EXPERT_GROUPS = 4
EXPERTS_PER_GROUP = 4
D_EXPERT = 256
SUBLANES = 8
LANES = 128

PROJ_TILE = 1536
COL_Z, COL_GA, COL_GB = 0, 1024, 2048
PROJ_WIDTH = 3072

V7X_VMEM_BYTES = 64 * 1024 * 1024
VMEM_LIMIT = V7X_VMEM_BYTES * 7 // 8


def _tiles(s):
    return dict(
        in_proj=min(1024, s),
        attn=512,
        deltanet=min(512, s),
        merge=min(512, s),
        moe=min(1024, s),
        moe_slab=128,
        ple=min(1024, s),
    )


def _dot(a, b):
    return jnp.dot(a, b, preferred_element_type=F32)


def _dot_nt(a, b):
    return lax.dot_general(a, b, (((1,), (1,)), ((), ())), preferred_element_type=F32)


def _dot_tn(a, b):
    return lax.dot_general(a, b, (((0,), (0,)), ((), ())), preferred_element_type=F32)


def _split2(v):
    hi = v.astype(BF16)
    lo = (v - hi.astype(F32)).astype(BF16)
    return hi, lo


def _split3(v):
    a = v.astype(BF16)
    r = v - a.astype(F32)
    b = r.astype(BF16)
    c = (r - b.astype(F32)).astype(BF16)
    return a, b, c


def _sigmoid(v):
    return 0.5 + 0.5 * jnp.tanh(0.5 * v)


def _silu(v):
    half = 0.5 * v
    return half + half * jnp.tanh(half)


def _heads_to_lanes(per_head):
    lane = lax.broadcasted_iota(jnp.int32, per_head.shape, 1)
    heads_per_slab = LANES // HEAD_DIM_A
    outs = []
    for c in range(GROUP_WIDTH // LANES):
        idx = lane // HEAD_DIM_A + heads_per_slab * c
        outs.append(jnp.take_along_axis(per_head, idx, axis=1))
    return jnp.concatenate(outs, axis=1)


def _params(sem):
    return pltpu.CompilerParams(dimension_semantics=sem, vmem_limit_bytes=VMEM_LIMIT)


def _inproj_main_kernel(x_ref, gain_ref, w_ref, wab_ref, proj_ref, ab_ref, h_ref):
    @pl.when(pl.program_id(1) == 0)
    def _():
        x = x_ref[...]
        ms = jnp.mean(x * x, axis=-1, keepdims=True)
        h = x * lax.rsqrt(ms + EPS) * gain_ref[...]
        hi, lo = _split2(h)
        h_ref[...] = hi
        hi_terms = _dot(hi, wab_ref[...])
        ab_ref[...] = hi_terms[:, :LANES] + hi_terms[:, LANES:] + _dot(lo, wab_ref[:, :LANES])

    proj_ref[...] = _dot(h_ref[...], w_ref[...]).astype(BF16)


def _inproj_attn_kernel(h_ref, w_ref, qkgain_ref, red_ref, out_ref, perm_scr, *, tm, g):
    slabs = GROUP_WIDTH // LANES

    def normed(a, gain):
        ss = _dot((a * a).astype(BF16), red_ref[...])
        r = lax.rsqrt(ss * (1.0 / HEAD_DIM_A) + EPS)
        return a * _heads_to_lanes(r) * gain

    h = h_ref[...]
    dil = ATTN_GROUPS[g][1]
    n_u = tm // dil
    for p in range(3):
        pcols = slice(p * GROUP_WIDTH, (p + 1) * GROUP_WIDTH)
        val = _dot(h, w_ref[:, pcols])
        if p < 2:
            val = normed(val, qkgain_ref[:, pcols])
        if dil == 1:
            out_ref[0, :, pcols] = val.astype(BF16)
        else:
            if dil < SUBLANES:
                for c in range(slabs):
                    perm_scr[c, 0:tm, :] = val[:, c * LANES:(c + 1) * LANES]
                for r in range(dil):
                    for c in range(slabs):
                        lo = p * GROUP_WIDTH + c * LANES
                        out_ref[r, :, lo:lo + LANES] = (
                            perm_scr[c, pl.ds(r, n_u, stride=dil), :].astype(BF16))
            else:
                pitch = n_u + 1
                for c in range(slabs):
                    for m in range(tm // SUBLANES):
                        u, r0 = divmod(m * SUBLANES, dil)
                        perm_scr[c, pl.ds(r0 * pitch + u, SUBLANES, stride=pitch), :] = (
                            val[m * SUBLANES:(m + 1) * SUBLANES, c * LANES:(c + 1) * LANES])
                for r in range(dil):
                    for c in range(slabs):
                        lo = p * GROUP_WIDTH + c * LANES
                        out_ref[r, :, lo:lo + LANES] = perm_scr[c, pl.ds(r * pitch, n_u), :].astype(BF16)


def _in_projection(x2, norm_gain, w_plain, w_groups, wab, qkgain, b, s, tm):
    t, d = x2.shape
    assert w_plain.shape[1] == PROJ_WIDTH and PROJ_WIDTH % PROJ_TILE == 0
    proj, ab, h = pl.pallas_call(
        _inproj_main_kernel,
        grid=(t // tm, PROJ_WIDTH // PROJ_TILE),
        in_specs=[
            pl.BlockSpec((tm, d), lambda i, j: (i, 0)),
            pl.BlockSpec((1, d), lambda i, j: (0, 0)),
            pl.BlockSpec((d, PROJ_TILE), lambda i, j: (0, j)),
            pl.BlockSpec((d, 2 * LANES), lambda i, j: (0, 0)),
        ],
        out_specs=[
            pl.BlockSpec((tm, PROJ_TILE), lambda i, j: (i, j)),
            pl.BlockSpec((tm, LANES), lambda i, j: (i, 0)),
            pl.BlockSpec((tm, d), lambda i, j: (i, 0)),
        ],
        out_shape=[
            jax.ShapeDtypeStruct((t, PROJ_WIDTH), BF16),
            jax.ShapeDtypeStruct((t, LANES), F32),
            jax.ShapeDtypeStruct((t, d), BF16),
        ],
        compiler_params=_params(("parallel", "arbitrary")),
        name="in_proj",
    )(x2, norm_gain, w_plain, wab)

    tps = s // tm
    col = jnp.arange(GROUP_WIDTH) // HEAD_DIM_A
    red = (col[:, None] == jnp.arange(LANES)[None, :]).astype(BF16)
    qkv = []
    for g, (_, dil) in enumerate(ATTN_GROUPS):
        qkv.append(pl.pallas_call(
            functools.partial(_inproj_attn_kernel, tm=tm, g=g),
            grid=(t // tm,),
            in_specs=[
                pl.BlockSpec((tm, d), lambda i: (i, 0)),
                pl.BlockSpec((d, PROJ_TILE), lambda i: (0, 0)),
                pl.BlockSpec((1, 2 * GROUP_WIDTH), lambda i: (0, 0)),
                pl.BlockSpec((GROUP_WIDTH, LANES), lambda i: (0, 0)),
            ],
            out_specs=pl.BlockSpec((None, dil, tm // dil, PROJ_TILE), lambda i: (i // tps, 0, i % tps, 0)),
            out_shape=jax.ShapeDtypeStruct((b, dil, s // dil, PROJ_TILE), BF16),
            scratch_shapes=[pltpu.VMEM((GROUP_WIDTH // LANES, tm + dil, LANES), F32)],
            compiler_params=_params(("parallel",)),
            name=f"in_proj_attn_d{dil}",
        )(h, w_groups[g], qkgain, red))
    return proj, ab, h, qkv


def _attn_kernel(q_ref, kc_ref, kp_ref, vc_ref, vp_ref, o_ref, lse_ref, *, n_blk, pairs_at_once=4):
    n = pl.program_id(2)
    blk = ATTN_BLOCK
    qi = lax.broadcasted_iota(jnp.int32, (2 * blk, blk), 0) & (blk - 1)
    kj = lax.broadcasted_iota(jnp.int32, (2 * blk, blk), 1)
    cur_ok2 = kj <= qi
    prev_band = kj >= qi
    lane = lax.broadcasted_iota(jnp.int32, (blk, LANES), 1)
    low_half = lane < HEAD_DIM_A
    neg_inf = jnp.float32(-jnp.inf)
    ones = jnp.ones((blk, LANES), BF16)
    zero = jnp.zeros((blk, LANES), BF16)

    for b in range(n_blk):
        rows = slice(b * blk, (b + 1) * blk)
        if b == 0:
            prev_ok2 = jnp.logical_and(prev_band, n > 0)
        else:
            prev_ok2 = prev_band
        lse_tile = jnp.zeros((blk, LANES), F32)
        for p0 in range(0, HEADS_A // 2, pairs_at_once):
            pairs = range(p0, p0 + pairs_at_once)
            cols = [slice(pr * LANES, (pr + 1) * LANES) for pr in pairs]
            q2 = [q_ref[rows, c] for c in cols]
            kc2 = [kc_ref[rows, c] for c in cols]
            vc2 = [vc_ref[rows, c] for c in cols]
            if b == 0:
                kp2 = [kp_ref[:, c] for c in cols]
                vp2 = [vp_ref[:, c] for c in cols]
            else:
                prow = slice((b - 1) * blk, b * blk)
                kp2 = [kc_ref[prow, c] for c in cols]
                vp2 = [vc_ref[prow, c] for c in cols]
            q_st = [jnp.concatenate([jnp.where(low_half, q, zero), jnp.where(low_half, zero, q)], axis=0)
                    for q in q2]
            vc_aug = [jnp.concatenate([x, ones], axis=1) for x in vc2]
            vp_aug = [jnp.concatenate([x, ones], axis=1) for x in vp2]
            s_c = [jnp.where(cur_ok2, _dot_nt(q, kk), neg_inf) for q, kk in zip(q_st, kc2)]
            s_p = [jnp.where(prev_ok2, _dot_nt(q, kk), neg_inf) for q, kk in zip(q_st, kp2)]
            m = [jnp.maximum(jnp.max(sc, axis=-1, keepdims=True), jnp.max(sp, axis=-1, keepdims=True))
                 for sc, sp in zip(s_c, s_p)]
            e_c = [jnp.exp2(sc - mh).astype(BF16) for sc, mh in zip(s_c, m)]
            e_p = [jnp.exp2(sp - mh).astype(BF16) for sp, mh in zip(s_p, m)]
            nd = [_dot(ec, va) + _dot(ep, vb) for ec, va, ep, vb in zip(e_c, vc_aug, e_p, vp_aug)]
            outs = [x[:, :LANES] / x[:, LANES:] for x in nd]
            lse = [mh * LN2 + jnp.log(x[:, LANES:LANES + 1]) for mh, x in zip(m, nd)]
            for i, pr in enumerate(pairs):
                for half in range(2):
                    hrows = slice(half * blk, (half + 1) * blk)
                    lse_tile = jnp.where(lane == 2 * pr + half, lse[i][hrows], lse_tile)
                o_ref[rows, cols[i]] = jnp.where(low_half, outs[i][0:blk], outs[i][blk:2 * blk]).astype(BF16)
        lse_ref[rows, :] = lse_tile


def _attention_group(arr4, cq, ck, cv, tq):
    b, dil, n_sub, _ = arr4.shape
    tq = min(tq, n_sub)
    n_blk = tq // ATTN_BLOCK

    def cur(c0):
        return pl.BlockSpec((None, None, tq, GROUP_WIDTH), lambda bi, r, n: (bi, r, n, c0))

    def prev(c0):
        return pl.BlockSpec((None, None, ATTN_BLOCK, GROUP_WIDTH),
                            lambda bi, r, n: (bi, r, jnp.maximum(n * n_blk - 1, 0), c0))

    return pl.pallas_call(
        functools.partial(_attn_kernel, n_blk=n_blk),
        grid=(b, dil, n_sub // tq),
        in_specs=[cur(cq), cur(ck), prev(ck), cur(cv), prev(cv)],
        out_specs=[
            pl.BlockSpec((None, None, tq, GROUP_WIDTH), lambda bi, r, n: (bi, r, n, 0)),
            pl.BlockSpec((None, None, tq, LANES), lambda bi, r, n: (bi, r, n, 0)),
        ],
        out_shape=[
            jax.ShapeDtypeStruct((b, dil, n_sub, GROUP_WIDTH), BF16),
            jax.ShapeDtypeStruct((b, dil, n_sub, LANES), F32),
        ],
        compiler_params=_params(("parallel", "parallel", "parallel")),
        name=f"attn_d{dil}",
    )(arr4, arr4, arr4, arr4, arr4)


def _inproj_dn_kernel(h_ref, w_ref, convw_ref, out_ref, carry_scr, ext_scr, *, tm, tiles_per_seq):
    halo = SUBLANES
    part_w = 4 * DN_DIM

    @pl.when(pl.program_id(0) % tiles_per_seq == 0)
    def _():
        carry_scr[...] = jnp.zeros_like(carry_scr)

    h = h_ref[...]
    for part in range(3 * DN_WIDTH // part_w):
        acc = _dot(h, w_ref[:, part * part_w:(part + 1) * part_w])
        for s in range(part_w // DN_DIM):
            c = part * (part_w // DN_DIM) + s
            cols = slice(c * DN_DIM, (c + 1) * DN_DIM)
            cur = acc[:, s * DN_DIM:(s + 1) * DN_DIM]
            ext_scr[c, 0:halo, :] = carry_scr[:, cols]
            ext_scr[c, halo:halo + tm, :] = cur
            y = cur * convw_ref[CONV_WIDTH - 1:CONV_WIDTH, cols]
            for j in range(CONV_WIDTH - 1):
                off = halo - (CONV_WIDTH - 1) + j
                y = y + ext_scr[c, pl.ds(off, tm), :] * convw_ref[j:j + 1, cols]
            carry_scr[:, cols] = cur[tm - halo:tm]
            y = _silu(y)
            if c < 2 * DN_HEADS:
                inv = lax.rsqrt(jnp.sum(y * y, axis=-1, keepdims=True) + EPS)
                if c < DN_HEADS:
                    inv = inv * (DN_DIM ** -0.5)
                y = y * inv
            out_ref[:, cols] = y.astype(BF16)


def _in_projection_dn(h, w_dn, conv_w, s, tm):
    t, d = h.shape
    return pl.pallas_call(
        functools.partial(_inproj_dn_kernel, tm=tm, tiles_per_seq=s // tm),
        grid=(t // tm,),
        in_specs=[
            pl.BlockSpec((tm, d), lambda i: (i, 0)),
            pl.BlockSpec((d, 3 * DN_WIDTH), lambda i: (0, 0)),
            pl.BlockSpec((CONV_WIDTH, 3 * DN_WIDTH), lambda i: (0, 0)),
        ],
        out_specs=pl.BlockSpec((tm, 3 * DN_WIDTH), lambda i: (i, 0)),
        out_shape=jax.ShapeDtypeStruct((t, 3 * DN_WIDTH), BF16),
        scratch_shapes=[pltpu.VMEM((SUBLANES, 3 * DN_WIDTH), F32),
                        pltpu.VMEM((3 * DN_HEADS, SUBLANES + tm, DN_DIM), F32)],
        compiler_params=_params(("arbitrary",)),
        name="in_proj_dn",
    )(h, w_dn, conv_w)


def _deltanet_kernel(xb_ref, z_ref, ab_ref, alog_ref, dtb_ref, ognorm_ref, lcum_ref,
                     o_ref, gc_scr, beta_scr, state_scr,
                     u_scr, w_scr, qd_scr, kt_scr, attn_scr, *, tile, chunks_per_iter=8):
    t_idx = pl.program_id(1)
    n_chunks = tile // CHUNK
    heads = range(DN_HEADS)

    @pl.when(t_idx == 0)
    def _():
        state_scr[...] = jnp.zeros_like(state_scr)

    ab = ab_ref[...]
    beta_scr[...] = _sigmoid(ab)
    sp_in = ab + dtb_ref[...]
    softplus = jnp.maximum(sp_in, 0.0) + jnp.log(1.0 + jnp.exp(-jnp.abs(sp_in)))
    gdec = -jnp.exp(alog_ref[...]) * softplus
    g1, g2, g3 = _split3(gdec)
    lcum = lcum_ref[...]
    gc_scr[...] = _dot(lcum, g1) + _dot(lcum, g2) + _dot(lcum, g3)

    ri = lax.broadcasted_iota(jnp.int32, (CHUNK, CHUNK), 0)
    ci = lax.broadcasted_iota(jnp.int32, (CHUNK, CHUNK), 1)
    tril = ci <= ri
    tril_strict = ci < ri
    eye = (ci == ri).astype(F32)
    neg_inf = jnp.float32(-jnp.inf)

    def intra_body(c, carry):
        rows_c = [pl.ds(pl.multiple_of((c * chunks_per_iter + i) * CHUNK, CHUNK), CHUNK)
                  for i in range(chunks_per_iter)]
        units = [(i, h) for i in range(chunks_per_iter) for h in heads]
        gc_c = [gc_scr[r, :] for r in rows_c]
        gc_t_c = [jnp.transpose(g) for g in gc_c]
        beta_c = [beta_scr[r, :] for r in rows_c]
        k16 = [xb_ref[rows_c[i], DN_WIDTH + h * DN_DIM:DN_WIDTH + (h + 1) * DN_DIM] for i, h in units]
        q = [xb_ref[rows_c[i], h * DN_DIM:(h + 1) * DN_DIM].astype(F32) for i, h in units]
        k = [x.astype(F32) for x in k16]
        v = [xb_ref[rows_c[i], 2 * DN_WIDTH + h * DN_DIM:2 * DN_WIDTH + (h + 1) * DN_DIM].astype(F32)
             for i, h in units]

        b_col = [beta_c[i][:, h:h + 1] for i, h in units]
        g_col = [gc_c[i][:, DN_HEADS + h:DN_HEADS + h + 1] for i, h in units]
        g_row = [gc_t_c[i][DN_HEADS + h:DN_HEADS + h + 1, :] for i, h in units]
        g_last = [gcol[CHUNK - 1:CHUNK, :] for gcol in g_col]
        decay = [jnp.exp(jnp.where(tril, gcol - grow, neg_inf)) for gcol, grow in zip(g_col, g_row)]
        e_g = [jnp.exp(gcol) for gcol in g_col]
        kb = [kh * bh for kh, bh in zip(k, b_col)]
        a_mat = [jnp.where(tril_strict, _dot_nt(kbh.astype(BF16), kh16) * dh, 0.0)
                 for kbh, kh16, dh in zip(kb, k16, decay)]
        pw = [-a for a in a_mat]
        t_mat = [eye + p for p in pw]
        for _ in range(5):
            pw16 = [p.astype(BF16) for p in pw]
            pw = [_dot(p, p) for p in pw16]
            t_mat = [th + _dot(th.astype(BF16), p.astype(BF16)) for th, p in zip(t_mat, pw)]
        t16 = [th.astype(BF16) for th in t_mat]
        u = [_dot(th, (vh * bh).astype(BF16)) for th, vh, bh in zip(t16, v, b_col)]
        w = [_dot(th, (kbh * eh).astype(BF16)) for th, kbh, eh in zip(t16, kb, e_g)]
        attn = [jnp.where(tril, _dot_nt(qh.astype(BF16), kh16) * dh, 0.0).astype(BF16)
                for qh, kh16, dh in zip(q, k16, decay)]
        q_dec = [(qh * eh).astype(BF16) for qh, eh in zip(q, e_g)]
        k_tail = [(kh * jnp.exp(gl - gcol)).astype(BF16) for kh, gl, gcol in zip(k, g_last, g_col)]
        for n, (i, h) in enumerate(units):
            hc = slice(h * DN_DIM, (h + 1) * DN_DIM)
            u_scr[rows_c[i], hc] = u[n]
            w_scr[rows_c[i], hc] = w[n].astype(BF16)
            qd_scr[rows_c[i], hc] = q_dec[n]
            kt_scr[rows_c[i], hc] = k_tail[n]
            attn_scr[h, rows_c[i], :] = attn[n]
        return carry

    lax.fori_loop(0, n_chunks // chunks_per_iter, intra_body, 0)

    def scan_body(c, carry):
        r0 = pl.multiple_of(c * CHUNK, CHUNK)
        rows = pl.ds(r0, CHUNK)
        decay_last = jnp.exp(gc_scr[pl.ds(r0 + CHUNK - 1, 1), :])
        hcs = [slice(h * DN_DIM, (h + 1) * DN_DIM) for h in heads]
        state = [state_scr[h] for h in heads]
        u = [u_scr[rows, hc] for hc in hcs]
        w16 = [w_scr[rows, hc] for hc in hcs]
        qd = [qd_scr[rows, hc] for hc in hcs]
        kt = [kt_scr[rows, hc] for hc in hcs]
        attn = [attn_scr[h, rows, :] for h in heads]
        z = [z_ref[rows, hc] for hc in hcs]
        s16 = [sh.astype(BF16) for sh in state]
        v_new = [(uh - _dot(wh, sh)).astype(BF16) for uh, wh, sh in zip(u, w16, s16)]
        o = [_dot(qh, sh) + _dot(ah, vn) for qh, sh, ah, vn in zip(qd, s16, attn, v_new)]
        new_state = [sh * decay_last[:, DN_HEADS + h:DN_HEADS + h + 1] + _dot_tn(kth, vn)
                     for h, sh, kth, vn in zip(heads, state, kt, v_new)]
        o = [oh * lax.rsqrt(jnp.mean(oh * oh, axis=-1, keepdims=True) + EPS) * ognorm_ref[...] for oh in o]
        o = [(oh * _silu(zh.astype(F32))).astype(BF16) for oh, zh in zip(o, z)]
        for h in heads:
            state_scr[h] = new_state[h]
        for h in heads:
            o_ref[rows, hcs[h]] = o[h]
        return carry

    lax.fori_loop(0, n_chunks, scan_body, 0, unroll=4)


def _deltanet(qkvn3, proj3, ab3, a_log_l, dtb_l, og_norm, tile):
    b, s, _ = proj3.shape
    lcum = jnp.tril(jnp.ones((tile, tile), F32))
    same_chunk = (jnp.arange(tile)[:, None] // CHUNK) == (jnp.arange(tile)[None, :] // CHUNK)
    lcum = jnp.where(same_chunk, lcum, 0.0).astype(BF16)
    return pl.pallas_call(
        functools.partial(_deltanet_kernel, tile=tile),
        grid=(b, s // tile),
        in_specs=[
            pl.BlockSpec((None, tile, 3 * DN_WIDTH), lambda bi, t: (bi, t, 0)),
            pl.BlockSpec((None, tile, DN_WIDTH), lambda bi, t: (bi, t, COL_Z // DN_WIDTH)),
            pl.BlockSpec((None, tile, LANES), lambda bi, t: (bi, t, 0)),
            pl.BlockSpec((1, LANES), lambda bi, t: (0, 0)),
            pl.BlockSpec((1, LANES), lambda bi, t: (0, 0)),
            pl.BlockSpec((1, DN_DIM), lambda bi, t: (0, 0)),
            pl.BlockSpec((tile, tile), lambda bi, t: (0, 0)),
        ],
        out_specs=pl.BlockSpec((None, tile, DN_WIDTH), lambda bi, t: (bi, t, 0)),
        out_shape=jax.ShapeDtypeStruct((b, s, DN_WIDTH), BF16),
        scratch_shapes=[
            pltpu.VMEM((tile, LANES), F32),
            pltpu.VMEM((tile, LANES), F32),
            pltpu.VMEM((DN_HEADS, DN_DIM, DN_DIM), F32),
            pltpu.VMEM((tile, DN_WIDTH), F32),
            pltpu.VMEM((tile, DN_WIDTH), BF16),
            pltpu.VMEM((tile, DN_WIDTH), BF16),
            pltpu.VMEM((tile, DN_WIDTH), BF16),
            pltpu.VMEM((DN_HEADS, tile, CHUNK), BF16),
        ],
        compiler_params=_params(("parallel", "arbitrary")),
        name="deltanet",
    )(qkvn3, proj3, ab3, a_log_l, dtb_l, og_norm, lcum)


def _merge_kernel(o0_ref, o1_ref, o2_ref, l0_ref, l1_ref, l2_ref, ob_ref, ga_ref, gb_ref, x_ref,
                  wa_ref, wb_ref, wo_ref, hexp_ref, gain_ref, wr_ref, rbias_ref,
                  x1_ref, h2_ref, logit_ref, o1_scr, o2_scr, l1_scr, l2_scr, *, tm):
    for g, src, dst in ((1, o1_ref, o1_scr), (2, o2_ref, o2_scr), (1, l1_ref, l1_scr), (2, l2_ref, l2_scr)):
        dil = ATTN_GROUPS[g][1]
        for r in range(dil):
            for c in range(dst.shape[0]):
                dst[c, pl.ds(r, tm // dil, stride=dil), :] = src[r, :, c * LANES:(c + 1) * LANES].astype(F32)

    def slabs(scr):
        return jnp.concatenate([scr[c] for c in range(scr.shape[0])], axis=-1)

    l0, l1, l2 = l0_ref[...], l1_scr[0], l2_scr[0]
    m = jnp.maximum(jnp.maximum(l0, l1), l2)
    e0, e1, e2 = jnp.exp(l0 - m), jnp.exp(l1 - m), jnp.exp(l2 - m)
    tot = e0 + e1 + e2
    hexp = hexp_ref[...]
    o_a = jnp.zeros(o0_ref.shape, F32)
    for e, o_val in ((e0, o0_ref[...].astype(F32)), (e1, slabs(o1_scr)), (e2, slabs(o2_scr))):
        w_hi, w_lo = _split2(e / tot)
        o_a = o_a + _dot(jnp.concatenate([w_hi, w_lo], axis=1), hexp) * o_val
    ya = _dot(o_a.astype(BF16), wa_ref[...])
    yb = _dot(ob_ref[...], wb_ref[...])
    merged = _sigmoid(ga_ref[...].astype(F32)) * ya + _sigmoid(gb_ref[...].astype(F32)) * yb
    x1 = x_ref[...] + _dot(merged.astype(BF16), wo_ref[...])
    x1_ref[...] = x1
    h2 = x1 * lax.rsqrt(jnp.mean(x1 * x1, axis=-1, keepdims=True) + EPS) * gain_ref[...]
    hi, lo = _split2(h2)
    h2_ref[...] = hi
    hi_terms = _dot(hi, wr_ref[...])
    logit_ref[...] = (hi_terms[:, :LANES] + hi_terms[:, LANES:] + _dot(lo, wr_ref[:, :LANES])
                      + rbias_ref[...])


def _merge(o0, l0, o_perm, l_perm, o_b, proj, x2, wa, wb, wo, gain, wr, rbias, s, tm):
    t, d = x2.shape
    tps = s // tm
    head_of_col = jnp.arange(GROUP_WIDTH) // HEAD_DIM_A
    hexp = (jnp.arange(LANES)[:, None] == head_of_col[None, :]).astype(BF16)
    hexp = jnp.concatenate([hexp, hexp], axis=0)

    def row(wd, cblk=0):
        return pl.BlockSpec((tm, wd), lambda i: (i, cblk))

    def perm(arr):
        _, dil, _, wd = arr.shape
        return pl.BlockSpec((None, dil, tm // dil, wd), lambda i: (i // tps, 0, i % tps, 0))

    def full(a):
        return pl.BlockSpec(a.shape, lambda i: (0, 0))

    return pl.pallas_call(
        functools.partial(_merge_kernel, tm=tm),
        grid=(t // tm,),
        in_specs=[row(GROUP_WIDTH), perm(o_perm[0]), perm(o_perm[1]),
                  row(LANES), perm(l_perm[0]), perm(l_perm[1]),
                  row(DN_WIDTH), row(d, COL_GA // d), row(d, COL_GB // d), row(d),
                  full(wa), full(wb), full(wo), full(hexp), full(gain), full(wr),
                  full(rbias)],
        out_specs=[row(d), row(d), row(LANES)],
        out_shape=[
            jax.ShapeDtypeStruct((t, d), F32),
            jax.ShapeDtypeStruct((t, d), BF16),
            jax.ShapeDtypeStruct((t, LANES), F32),
        ],
        scratch_shapes=[pltpu.VMEM((GROUP_WIDTH // LANES, tm, LANES), F32),
                        pltpu.VMEM((GROUP_WIDTH // LANES, tm, LANES), F32),
                        pltpu.VMEM((1, tm, LANES), F32), pltpu.VMEM((1, tm, LANES), F32)],
        compiler_params=_params(("parallel",)),
        name="merge",
    )(o0, o_perm[0], o_perm[1], l0, l_perm[0], l_perm[1], o_b, proj, proj, x2,
      wa, wb, wo, hexp, gain, wr, rbias)


def _routing_weights(logits):
    lane_i = lax.broadcasted_iota(jnp.int32, logits.shape, 1)
    lane = lane_i.astype(F32)
    neg_inf = jnp.float32(-jnp.inf)
    big = jnp.float32(1 << 20)
    is_group = lane_i < N_EXPERT_GROUPS
    gl = jnp.where(is_group, logits, neg_inf)
    gmax = jnp.max(gl, axis=-1, keepdims=True)
    gsum = jnp.sum(jnp.exp(gl - gmax), axis=-1, keepdims=True)
    p_g = 1.0 / gsum
    g_idx = jnp.min(jnp.where(gl == gmax, lane, big), axis=-1, keepdims=True)
    n_exp = N_EXPERT_GROUPS * EXPERTS_PER_GROUP
    is_exp = jnp.logical_and(lane_i >= N_EXPERT_GROUPS, lane_i < N_EXPERT_GROUPS + n_exp)
    exp_group = jnp.right_shift(lane_i - N_EXPERT_GROUPS, 2).astype(F32)
    assert EXPERTS_PER_GROUP == 4
    sel = jnp.logical_and(is_exp, exp_group == g_idx)
    el = jnp.where(sel, logits, neg_inf)
    emax = jnp.max(el, axis=-1, keepdims=True)
    ee = jnp.exp(el - emax)
    esum = jnp.sum(ee, axis=-1, keepdims=True)
    idx1 = jnp.min(jnp.where(el == emax, lane, big), axis=-1, keepdims=True)
    el2 = jnp.where(lane == idx1, neg_inf, el)
    e2max = jnp.max(el2, axis=-1, keepdims=True)
    idx2 = jnp.min(jnp.where(el2 == e2max, lane, big), axis=-1, keepdims=True)
    p1 = 1.0 / esum
    p2 = jnp.exp(e2max - emax) / esum
    top_sum = p1 + p2
    w = jnp.where(lane == idx1, p1 / top_sum, jnp.where(lane == idx2, p2 / top_sum, 0.0))
    return w * p_g, g_idx


def _moe_kernel(h2_ref, logit_ref, x1_ref, wg_ref, wu_ref, wd_ref, lstrict_ref, out_ref,
                hs_scr, cws_scr, pos_scr, acc_scr, off_smem, *, tm, sub):
    g = pl.program_id(1)
    lane = lax.broadcasted_iota(jnp.int32, (tm, LANES), 1)

    @pl.when(g == 0)
    def _():
        cw, g_idx = _routing_weights(logit_ref[...])
        onehot = jnp.where(lane.astype(F32) == g_idx, 1.0, 0.0)
        earlier = _dot(lstrict_ref[...], onehot.astype(BF16))
        counts = jnp.sum(onehot, axis=0, keepdims=True)
        lane_row = lax.broadcasted_iota(jnp.int32, (1, LANES), 1)
        start = jnp.zeros((1, LANES), F32)
        for k in range(N_EXPERT_GROUPS - 1):
            start = start + jnp.where(lane_row > k, counts[:, k:k + 1], 0.0)
        pos = jnp.sum(onehot * (earlier + start), axis=-1, keepdims=True)
        pos_b = jnp.broadcast_to(pos, (tm, LANES))
        pos_scr[...] = pos_b
        pos_row = jnp.transpose(pos_b)[0:1, :]
        dest = lax.broadcasted_iota(jnp.int32, (tm, tm), 0).astype(F32)
        gather = jnp.where(dest == pos_row, 1.0, 0.0).astype(BF16)
        hs_scr[...] = _dot(gather, h2_ref[...]).astype(BF16)
        cw_sorted = _dot(gather, jnp.concatenate(_split2(cw), axis=1))
        cws_scr[...] = cw_sorted[:, :LANES] + cw_sorted[:, LANES:]
        for k in range(N_EXPERT_GROUPS):
            off_smem[k] = start[0, k].astype(jnp.int32)
        off_smem[N_EXPERT_GROUPS] = jnp.int32(tm)
        acc_scr[...] = jnp.zeros_like(acc_scr)

    first = off_smem[g]
    last = off_smem[g + 1]
    lane_s = lax.broadcasted_iota(jnp.int32, (sub, LANES), 1)
    for s in range(tm // sub):
        @pl.when(jnp.logical_and(first < (s + 1) * sub, last > s * sub))
        def _(s=s):
            rows = slice(s * sub, (s + 1) * sub)
            hs = hs_scr[rows, :]
            hg = _dot(hs, wg_ref[...])
            hu = _dot(hs, wu_ref[...])
            cw = cws_scr[rows, :]
            parts = []
            for j in range(EXPERTS_PER_GROUP):
                cols = slice(j * D_EXPERT, (j + 1) * D_EXPERT)
                want = N_EXPERT_GROUPS + g * EXPERTS_PER_GROUP + j
                c_col = jnp.sum(jnp.where(lane_s == want, cw, 0.0), axis=-1, keepdims=True)
                parts.append((_silu(hg[:, cols]) * hu[:, cols] * c_col).astype(BF16))
            act = jnp.concatenate(parts, axis=-1)
            acc_scr[rows, :] += _dot(act, wd_ref[...])

    @pl.when(g == N_EXPERT_GROUPS - 1)
    def _():
        src = lax.broadcasted_iota(jnp.int32, (tm, tm), 1).astype(F32)
        scatter = jnp.where(pos_scr[:, 0:1] == src, 1.0, 0.0).astype(BF16)
        out_ref[...] = x1_ref[...] + _dot(scatter, acc_scr[...].astype(BF16))


def _moe(h2, logits, x1, wg, wu, wd, tm, sub):
    t, d = x1.shape
    gw = EXPERTS_PER_GROUP * D_EXPERT
    lstrict = jnp.tril(jnp.ones((tm, tm), F32), -1).astype(BF16)
    return pl.pallas_call(
        functools.partial(_moe_kernel, tm=tm, sub=sub),
        grid=(t // tm, N_EXPERT_GROUPS),
        in_specs=[
            pl.BlockSpec((tm, d), lambda i, g: (i, 0)),
            pl.BlockSpec((tm, LANES), lambda i, g: (i, 0)),
            pl.BlockSpec((tm, d), lambda i, g: (i, 0)),
            pl.BlockSpec((None, d, gw), lambda i, g: (g, 0, 0)),
            pl.BlockSpec((None, d, gw), lambda i, g: (g, 0, 0)),
            pl.BlockSpec((None, gw, d), lambda i, g: (g, 0, 0)),
            pl.BlockSpec((tm, tm), lambda i, g: (0, 0)),
        ],
        out_specs=pl.BlockSpec((tm, d), lambda i, g: (i, 0)),
        out_shape=jax.ShapeDtypeStruct((t, d), F32),
        scratch_shapes=[pltpu.VMEM((tm, d), BF16), pltpu.VMEM((tm, LANES), F32),
                        pltpu.VMEM((tm, LANES), F32), pltpu.VMEM((tm, d), F32),
                        pltpu.SMEM((8,), jnp.int32)],
        compiler_params=_params(("parallel", "arbitrary")),
        name="moe",
    )(h2, logits, x1, wg, wu, wd, lstrict)


def _ple_kernel(x_ref, p_ref, gain_ref, wpg_ref, wple_ref, out_ref):
    x = x_ref[...]
    h3 = x * lax.rsqrt(jnp.mean(x * x, axis=-1, keepdims=True) + EPS) * gain_ref[...]
    gate = _sigmoid(_dot(h3.astype(BF16), wpg_ref[...]))
    ple = _dot(p_ref[...].astype(BF16), wple_ref[...])
    out_ref[...] = x + gate * ple


def _ple(x2, p2, gain, wpg, wple, tm):
    t, d = x2.shape
    pd = p2.shape[1]
    return pl.pallas_call(
        _ple_kernel,
        grid=(t // tm,),
        in_specs=[
            pl.BlockSpec((tm, d), lambda i: (i, 0)),
            pl.BlockSpec((tm, pd), lambda i: (i, 0)),
            pl.BlockSpec((1, d), lambda i: (0, 0)),
            pl.BlockSpec((d, d), lambda i: (0, 0)),
            pl.BlockSpec((pd, d), lambda i: (0, 0)),
        ],
        out_specs=pl.BlockSpec((tm, d), lambda i: (i, 0)),
        out_shape=jax.ShapeDtypeStruct((t, d), F32),
        compiler_params=_params(("parallel",)),
        name="ple",
    )(x2, p2, gain, wpg, wple)


def _lane_pad(v, offset):
    out = jnp.zeros((1, LANES), F32)
    return out.at[0, offset:offset + v.shape[0]].set(v.astype(F32))


def _layer(x2, p2, b, s, norm_mix, w_in, q_norm, k_norm, conv_w, a_log, dt_bias, dn_out_norm,
           w_branch_a, w_branch_b, w_out, norm_ffn, w_router_group, b_router_group,
           w_router_expert, b_router_expert, w_expert_gate, w_expert_up, w_expert_down,
           norm_ple, w_ple, w_ple_gate):
    t, d = x2.shape
    tiles = _tiles(s)
    qa = N_GROUPS * GROUP_WIDTH
    o_b0 = 3 * qa
    o_z = o_b0 + 3 * DN_WIDTH
    o_beta = o_z + DN_WIDTH
    o_alpha = o_beta + DN_HEADS
    o_ga = o_alpha + DN_HEADS
    o_gb = o_ga + d
    w_in = w_in.astype(F32)
    gw_a = GROUP_WIDTH
    w_groups = [jnp.concatenate([w_in[:, part * qa + g * gw_a:part * qa + (g + 1) * gw_a]
                                 for part in range(3)], axis=1).astype(BF16) for g in range(N_GROUPS)]
    w_dn = w_in[:, o_b0:o_z].astype(BF16)
    w_plain = jnp.concatenate([
        w_in[:, o_z:o_beta],
        w_in[:, o_ga:o_gb],
        w_in[:, o_gb:o_gb + d],
    ], axis=1).astype(BF16)
    w_ab = jnp.zeros((d, LANES), F32).at[:, 0:2 * DN_HEADS].set(w_in[:, o_beta:o_ga])
    wab = jnp.concatenate(_split2(w_ab), axis=1)
    qg = jnp.tile(q_norm.astype(F32), HEADS_A) * (HEAD_DIM_A ** -0.5 * LOG2E)
    kg = jnp.tile(k_norm.astype(F32), HEADS_A)
    qkgain = jnp.concatenate([qg, kg])[None, :]

    proj, ab, h, qkv = _in_projection(x2, norm_mix.astype(F32)[None, :], w_plain, w_groups, wab,
                                      qkgain, b, s, tm=tiles["in_proj"])
    proj3 = proj.reshape(b, s, PROJ_WIDTH)
    qkv_dn = _in_projection_dn(h, w_dn, conv_w.astype(F32), s, tm=tiles["in_proj"])

    (o0, l0), (o1, l1), (o2, l2) = [_attention_group(a, 0, 1, 2, tq=tiles["attn"]) for a in qkv]

    o_b = _deltanet(qkv_dn.reshape(b, s, 3 * DN_WIDTH), proj3, ab.reshape(b, s, LANES),
                    _lane_pad(a_log, DN_HEADS), _lane_pad(dt_bias, DN_HEADS),
                    dn_out_norm.astype(F32)[None, :], tile=tiles["deltanet"]).reshape(t, DN_WIDTH)

    w_r = jnp.zeros((d, LANES), F32)
    w_r = w_r.at[:, 0:N_EXPERT_GROUPS].set(w_router_group.astype(F32))
    n_exp = N_EXPERT_GROUPS * EXPERTS_PER_GROUP
    w_r = w_r.at[:, N_EXPERT_GROUPS:N_EXPERT_GROUPS + n_exp].set(w_router_expert.astype(F32))
    w_r2 = jnp.concatenate(_split2(w_r), axis=1)
    rbias = _lane_pad(jnp.concatenate([b_router_group.astype(F32), b_router_expert.astype(F32)]), 0)

    x1, h2, logits = _merge(o0.reshape(t, GROUP_WIDTH), l0.reshape(t, LANES), (o1, o2), (l1, l2), o_b,
                            proj, x2, w_branch_a.astype(BF16), w_branch_b.astype(BF16),
                            w_out.astype(BF16), norm_ffn.astype(F32)[None, :], w_r2, rbias,
                            s, tm=tiles["merge"])

    gw = EXPERTS_PER_GROUP * D_EXPERT
    wg = jnp.transpose(w_expert_gate, (0, 2, 1, 3)).reshape(N_EXPERT_GROUPS, d, gw).astype(BF16)
    wu = jnp.transpose(w_expert_up, (0, 2, 1, 3)).reshape(N_EXPERT_GROUPS, d, gw).astype(BF16)
    wd = w_expert_down.reshape(N_EXPERT_GROUPS, gw, d).astype(BF16)
    x2n = _moe(h2, logits, x1, wg, wu, wd, tm=tiles["moe"], sub=tiles["moe_slab"])

    return _ple(x2n, p2, norm_ple.astype(F32)[None, :], w_ple_gate.astype(BF16), w_ple.astype(BF16),
                tm=tiles["ple"])


def kernel(x, p, norm_mix, w_in, q_norm, k_norm, conv_w, a_log, dt_bias, dn_out_norm, w_branch_a, w_branch_b, w_out, norm_ffn, w_router_group, b_router_group, w_router_expert, b_router_expert, w_expert_gate, w_expert_up, w_expert_down, norm_ple, w_ple, w_ple_gate):
    b, s, d = x.shape
    depth = w_in.shape[0]
    x2 = x.astype(F32).reshape(b * s, d)
    for i in range(depth):
        x2 = _layer(x2, p[i].reshape(b * s, -1), b, s, norm_mix[i], w_in[i], q_norm[i], k_norm[i],
                    conv_w[i], a_log[i], dt_bias[i], dn_out_norm[i], w_branch_a[i], w_branch_b[i],
                    w_out[i], norm_ffn[i], w_router_group[i], b_router_group[i], w_router_expert[i],
                    b_router_expert[i], w_expert_gate[i], w_expert_up[i], w_expert_down[i],
                    norm_ple[i], w_ple[i], w_ple_gate[i])
    return x2.reshape(b, s, d)
```

```python
import functools

import jax
import jax.numpy as jnp
from jax import lax
from jax.experimental import pallas as pl
from jax.experimental.pallas import tpu as pltpu

F32 = jnp.float32
BF16 = jnp.bfloat16
EPS = 1e-6
LN2 = 0.6931471805599453
LOG2E = 1.4426950408889634

ATTN_GROUPS = ((128, 1), (512, 4), (2048, 16))
N_GROUPS = 3
HEADS_A = 8
HEAD_DIM_A = 64
GROUP_WIDTH = HEADS_A * HEAD_DIM_A
ATTN_BLOCK = 128
DN_HEADS = 8
DN_DIM = 128
DN_WIDTH = DN_HEADS * DN_DIM
CONV_WIDTH = 4
CHUNK = 64
N_EXPERT_GROUPS = 4
EXPERTS_PER_GROUP = 4
D_EXPERT = 256
SUBLANES = 8
LANES = 128

PROJ_TILE = 1536
COL_Z, COL_GA, COL_GB = 0, 1024, 2048
PROJ_WIDTH = 3072

V7X_VMEM_BYTES = 64 * 1024 * 1024
VMEM_LIMIT = V7X_VMEM_BYTES * 7 // 8


def _tiles(s):
    return dict(
        in_proj=min(1024, s),
        attn=1024,
        deltanet=min(512, s),
        merge=min(512, s),
        moe=min(1024, s),
        moe_slab=128,
        ple=min(1024, s),
    )


def _dot(a, b):
    return jnp.dot(a, b, preferred_element_type=F32)


def _dot_nt(a, b):
    return lax.dot_general(a, b, (((1,), (1,)), ((), ())), preferred_element_type=F32)


def _dot_tn(a, b):
    return lax.dot_general(a, b, (((0,), (0,)), ((), ())), preferred_element_type=F32)


def _split2(v):
    hi = v.astype(BF16)
    lo = (v - hi.astype(F32)).astype(BF16)
    return hi, lo


def _split3(v):
    a = v.astype(BF16)
    r = v - a.astype(F32)
    b = r.astype(BF16)
    c = (r - b.astype(F32)).astype(BF16)
    return a, b, c


def _sigmoid(v):
    return 0.5 + 0.5 * jnp.tanh(0.5 * v)


def _silu(v):
    half = 0.5 * v
    return half + half * jnp.tanh(half)


def _heads_to_lanes(per_head):
    lane = lax.broadcasted_iota(jnp.int32, per_head.shape, 1)
    heads_per_slab = LANES // HEAD_DIM_A
    outs = []
    for c in range(GROUP_WIDTH // LANES):
        idx = lane // HEAD_DIM_A + heads_per_slab * c
        outs.append(jnp.take_along_axis(per_head, idx, axis=1))
    return jnp.concatenate(outs, axis=1)


def _params(sem):
    return pltpu.CompilerParams(dimension_semantics=sem, vmem_limit_bytes=VMEM_LIMIT)


def _inproj_main_kernel(x_ref, gain_ref, w_ref, wab_ref, proj_ref, ab_ref, h_ref):
    @pl.when(pl.program_id(1) == 0)
    def _():
        x = x_ref[...]
        ms = jnp.mean(x * x, axis=-1, keepdims=True)
        h = x * lax.rsqrt(ms + EPS) * gain_ref[...]
        hi, lo = _split2(h)
        h_ref[...] = hi
        hi_terms = _dot(hi, wab_ref[...])
        ab_ref[...] = hi_terms[:, :LANES] + hi_terms[:, LANES:] + _dot(lo, wab_ref[:, :LANES])

    proj_ref[...] = _dot(h_ref[...], w_ref[...]).astype(BF16)


def _inproj_attn_kernel(h_ref, w_ref, qkgain_ref, red_ref, out_ref, perm_scr, *, tm, g):
    slabs = GROUP_WIDTH // LANES

    def normed(a, gain):
        ss = _dot((a * a).astype(BF16), red_ref[...])
        r = lax.rsqrt(ss * (1.0 / HEAD_DIM_A) + EPS)
        return a * _heads_to_lanes(r) * gain

    h = h_ref[...]
    dil = ATTN_GROUPS[g][1]
    n_u = tm // dil
    for p in range(3):
        pcols = slice(p * GROUP_WIDTH, (p + 1) * GROUP_WIDTH)
        val = _dot(h, w_ref[:, pcols])
        if p < 2:
            val = normed(val, qkgain_ref[:, pcols])
        if dil == 1:
            out_ref[0, :, pcols] = val.astype(BF16)
        else:
            if dil < SUBLANES:
                for c in range(slabs):
                    perm_scr[c, 0:tm, :] = val[:, c * LANES:(c + 1) * LANES]
                for r in range(dil):
                    for c in range(slabs):
                        lo = p * GROUP_WIDTH + c * LANES
                        out_ref[r, :, lo:lo + LANES] = (
                            perm_scr[c, pl.ds(r, n_u, stride=dil), :].astype(BF16))
            else:
                pitch = n_u + 1
                for c in range(slabs):
                    for m in range(tm // SUBLANES):
                        u, r0 = divmod(m * SUBLANES, dil)
                        perm_scr[c, pl.ds(r0 * pitch + u, SUBLANES, stride=pitch), :] = (
                            val[m * SUBLANES:(m + 1) * SUBLANES, c * LANES:(c + 1) * LANES])
                for r in range(dil):
                    for c in range(slabs):
                        lo = p * GROUP_WIDTH + c * LANES
                        out_ref[r, :, lo:lo + LANES] = perm_scr[c, pl.ds(r * pitch, n_u), :].astype(BF16)


def _in_projection(x2, norm_gain, w_plain, w_groups, wab, qkgain, b, s, tm):
    t, d = x2.shape
    assert w_plain.shape[1] == PROJ_WIDTH and PROJ_WIDTH % PROJ_TILE == 0
    proj, ab, h = pl.pallas_call(
        _inproj_main_kernel,
        grid=(t // tm, PROJ_WIDTH // PROJ_TILE),
        in_specs=[
            pl.BlockSpec((tm, d), lambda i, j: (i, 0)),
            pl.BlockSpec((1, d), lambda i, j: (0, 0)),
            pl.BlockSpec((d, PROJ_TILE), lambda i, j: (0, j)),
            pl.BlockSpec((d, 2 * LANES), lambda i, j: (0, 0)),
        ],
        out_specs=[
            pl.BlockSpec((tm, PROJ_TILE), lambda i, j: (i, j)),
            pl.BlockSpec((tm, LANES), lambda i, j: (i, 0)),
            pl.BlockSpec((tm, d), lambda i, j: (i, 0)),
        ],
        out_shape=[
            jax.ShapeDtypeStruct((t, PROJ_WIDTH), BF16),
            jax.ShapeDtypeStruct((t, LANES), F32),
            jax.ShapeDtypeStruct((t, d), BF16),
        ],
        compiler_params=_params(("parallel", "arbitrary")),
        name="in_proj",
    )(x2, norm_gain, w_plain, wab)

    tps = s // tm
    col = jnp.arange(GROUP_WIDTH) // HEAD_DIM_A
    red = (col[:, None] == jnp.arange(LANES)[None, :]).astype(BF16)
    qkv = []
    for g, (_, dil) in enumerate(ATTN_GROUPS):
        qkv.append(pl.pallas_call(
            functools.partial(_inproj_attn_kernel, tm=tm, g=g),
            grid=(t // tm,),
            in_specs=[
                pl.BlockSpec((tm, d), lambda i: (i, 0)),
                pl.BlockSpec((d, PROJ_TILE), lambda i: (0, 0)),
                pl.BlockSpec((1, 2 * GROUP_WIDTH), lambda i: (0, 0)),
                pl.BlockSpec((GROUP_WIDTH, LANES), lambda i: (0, 0)),
            ],
            out_specs=pl.BlockSpec((None, dil, tm // dil, PROJ_TILE), lambda i: (i // tps, 0, i % tps, 0)),
            out_shape=jax.ShapeDtypeStruct((b, dil, s // dil, PROJ_TILE), BF16),
            scratch_shapes=[pltpu.VMEM((GROUP_WIDTH // LANES, tm + dil, LANES), F32)],
            compiler_params=_params(("parallel",)),
            name=f"in_proj_attn_d{dil}",
        )(h, w_groups[g], qkgain, red))
    return proj, ab, h, qkv


def _attn_kernel(q_ref, kc_ref, kp_ref, vc_ref, vp_ref, o_ref, lse_ref, *, n_blk, pairs_at_once=4):
    n = pl.program_id(2)
    blk = ATTN_BLOCK
    qi = lax.broadcasted_iota(jnp.int32, (2 * blk, blk), 0) & (blk - 1)
    kj = lax.broadcasted_iota(jnp.int32, (2 * blk, blk), 1)
    cur_ok2 = kj <= qi
    prev_band = kj >= qi
    lane = lax.broadcasted_iota(jnp.int32, (blk, LANES), 1)
    low_half = lane < HEAD_DIM_A
    neg_inf = jnp.float32(-jnp.inf)
    ones = jnp.ones((blk, LANES), BF16)
    zero = jnp.zeros((blk, LANES), BF16)

    for b in range(n_blk):
        rows = slice(b * blk, (b + 1) * blk)
        if b == 0:
            prev_ok2 = jnp.logical_and(prev_band, n > 0)
        else:
            prev_ok2 = prev_band
        lse_tile = jnp.zeros((blk, LANES), F32)
        for p0 in range(0, HEADS_A // 2, pairs_at_once):
            pairs = range(p0, p0 + pairs_at_once)
            cols = [slice(pr * LANES, (pr + 1) * LANES) for pr in pairs]
            q2 = [q_ref[rows, c] for c in cols]
            kc2 = [kc_ref[rows, c] for c in cols]
            vc2 = [vc_ref[rows, c] for c in cols]
            if b == 0:
                kp2 = [kp_ref[:, c] for c in cols]
                vp2 = [vp_ref[:, c] for c in cols]
            else:
                prow = slice((b - 1) * blk, b * blk)
                kp2 = [kc_ref[prow, c] for c in cols]
                vp2 = [vc_ref[prow, c] for c in cols]
            q_st = [jnp.concatenate([jnp.where(low_half, q, zero), jnp.where(low_half, zero, q)], axis=0)
                    for q in q2]
            vc_aug = [jnp.concatenate([x, ones], axis=1) for x in vc2]
            vp_aug = [jnp.concatenate([x, ones], axis=1) for x in vp2]
            s_c = [jnp.where(cur_ok2, _dot_nt(q, kk), neg_inf) for q, kk in zip(q_st, kc2)]
            s_p = [jnp.where(prev_ok2, _dot_nt(q, kk), neg_inf) for q, kk in zip(q_st, kp2)]
            m = [jnp.maximum(jnp.max(sc, axis=-1, keepdims=True), jnp.max(sp, axis=-1, keepdims=True))
                 for sc, sp in zip(s_c, s_p)]
            e_c = [jnp.exp2(sc - mh).astype(BF16) for sc, mh in zip(s_c, m)]
            e_p = [jnp.exp2(sp - mh).astype(BF16) for sp, mh in zip(s_p, m)]
            nd = [_dot(ec, va) + _dot(ep, vb) for ec, va, ep, vb in zip(e_c, vc_aug, e_p, vp_aug)]
            outs = [x[:, :LANES] / x[:, LANES:] for x in nd]
            lse = [mh * LN2 + jnp.log(x[:, LANES:LANES + 1]) for mh, x in zip(m, nd)]
            for i, pr in enumerate(pairs):
                for half in range(2):
                    hrows = slice(half * blk, (half + 1) * blk)
                    lse_tile = jnp.where(lane == 2 * pr + half, lse[i][hrows], lse_tile)
                o_ref[rows, cols[i]] = jnp.where(low_half, outs[i][0:blk], outs[i][blk:2 * blk]).astype(BF16)
        lse_ref[rows, :] = lse_tile


def _attention_group(arr4, cq, ck, cv, tq):
    b, dil, n_sub, _ = arr4.shape
    tq = min(tq, n_sub)
    n_blk = tq // ATTN_BLOCK

    def cur(c0):
        return pl.BlockSpec((None, None, tq, GROUP_WIDTH), lambda bi, r, n: (bi, r, n, c0))

    def prev(c0):
        return pl.BlockSpec((None, None, ATTN_BLOCK, GROUP_WIDTH),
                            lambda bi, r, n: (bi, r, jnp.maximum(n * n_blk - 1, 0), c0))

    return pl.pallas_call(
        functools.partial(_attn_kernel, n_blk=n_blk),
        grid=(b, dil, n_sub // tq),
        in_specs=[cur(cq), cur(ck), prev(ck), cur(cv), prev(cv)],
        out_specs=[
            pl.BlockSpec((None, None, tq, GROUP_WIDTH), lambda bi, r, n: (bi, r, n, 0)),
            pl.BlockSpec((None, None, tq, LANES), lambda bi, r, n: (bi, r, n, 0)),
        ],
        out_shape=[
            jax.ShapeDtypeStruct((b, dil, n_sub, GROUP_WIDTH), BF16),
            jax.ShapeDtypeStruct((b, dil, n_sub, LANES), F32),
        ],
        compiler_params=_params(("parallel", "parallel", "parallel")),
        name=f"attn_d{dil}",
    )(arr4, arr4, arr4, arr4, arr4)


def _inproj_dn_kernel(h_ref, w_ref, convw_ref, out_ref, carry_scr, ext_scr, *, tm, tiles_per_seq):
    halo = SUBLANES
    part_w = 4 * DN_DIM

    @pl.when(pl.program_id(0) % tiles_per_seq == 0)
    def _():
        carry_scr[...] = jnp.zeros_like(carry_scr)

    h = h_ref[...]
    for part in range(3 * DN_WIDTH // part_w):
        acc = _dot(h, w_ref[:, part * part_w:(part + 1) * part_w])
        for s in range(part_w // DN_DIM):
            c = part * (part_w // DN_DIM) + s
            cols = slice(c * DN_DIM, (c + 1) * DN_DIM)
            cur = acc[:, s * DN_DIM:(s + 1) * DN_DIM]
            ext_scr[c, 0:halo, :] = carry_scr[:, cols]
            ext_scr[c, halo:halo + tm, :] = cur
            y = cur * convw_ref[CONV_WIDTH - 1:CONV_WIDTH, cols]
            for j in range(CONV_WIDTH - 1):
                off = halo - (CONV_WIDTH - 1) + j
                y = y + ext_scr[c, pl.ds(off, tm), :] * convw_ref[j:j + 1, cols]
            carry_scr[:, cols] = cur[tm - halo:tm]
            y = _silu(y)
            if c < 2 * DN_HEADS:
                inv = lax.rsqrt(jnp.sum(y * y, axis=-1, keepdims=True) + EPS)
                if c < DN_HEADS:
                    inv = inv * (DN_DIM ** -0.5)
                y = y * inv
            out_ref[:, cols] = y.astype(BF16)


def _in_projection_dn(h, w_dn, conv_w, s, tm):
    t, d = h.shape
    return pl.pallas_call(
        functools.partial(_inproj_dn_kernel, tm=tm, tiles_per_seq=s // tm),
        grid=(t // tm,),
        in_specs=[
            pl.BlockSpec((tm, d), lambda i: (i, 0)),
            pl.BlockSpec((d, 3 * DN_WIDTH), lambda i: (0, 0)),
            pl.BlockSpec((CONV_WIDTH, 3 * DN_WIDTH), lambda i: (0, 0)),
        ],
        out_specs=pl.BlockSpec((tm, 3 * DN_WIDTH), lambda i: (i, 0)),
        out_shape=jax.ShapeDtypeStruct((t, 3 * DN_WIDTH), BF16),
        scratch_shapes=[pltpu.VMEM((SUBLANES, 3 * DN_WIDTH), F32),
                        pltpu.VMEM((3 * DN_HEADS, SUBLANES + tm, DN_DIM), F32)],
        compiler_params=_params(("arbitrary",)),
        name="in_proj_dn",
    )(h, w_dn, conv_w)


def _deltanet_kernel(xb_ref, z_ref, ab_ref, alog_ref, dtb_ref, ognorm_ref, lcum_ref,
                     o_ref, gc_scr, beta_scr, state_scr,
                     u_scr, w_scr, qd_scr, kt_scr, attn_scr, *, tile, chunks_per_iter=8):
    t_idx = pl.program_id(1)
    n_chunks = tile // CHUNK
    heads = range(DN_HEADS)

    @pl.when(t_idx == 0)
    def _():
        state_scr[...] = jnp.zeros_like(state_scr)

    ab = ab_ref[...]
    beta_scr[...] = _sigmoid(ab)
    sp_in = ab + dtb_ref[...]
    softplus = jnp.maximum(sp_in, 0.0) + jnp.log(1.0 + jnp.exp(-jnp.abs(sp_in)))
    gdec = -jnp.exp(alog_ref[...]) * softplus
    g1, g2, g3 = _split3(gdec)
    lcum = lcum_ref[...]
    gc_scr[...] = _dot(lcum, g1) + _dot(lcum, g2) + _dot(lcum, g3)

    ri = lax.broadcasted_iota(jnp.int32, (CHUNK, CHUNK), 0)
    ci = lax.broadcasted_iota(jnp.int32, (CHUNK, CHUNK), 1)
    tril = ci <= ri
    tril_strict = ci < ri
    eye = (ci == ri).astype(F32)
    neg_inf = jnp.float32(-jnp.inf)

    def intra_body(c, carry):
        rows_c = [pl.ds(pl.multiple_of((c * chunks_per_iter + i) * CHUNK, CHUNK), CHUNK)
                  for i in range(chunks_per_iter)]
        units = [(i, h) for i in range(chunks_per_iter) for h in heads]
        gc_c = [gc_scr[r, :] for r in rows_c]
        gc_t_c = [jnp.transpose(g) for g in gc_c]
        beta_c = [beta_scr[r, :] for r in rows_c]
        k16 = [xb_ref[rows_c[i], DN_WIDTH + h * DN_DIM:DN_WIDTH + (h + 1) * DN_DIM] for i, h in units]
        q = [xb_ref[rows_c[i], h * DN_DIM:(h + 1) * DN_DIM].astype(F32) for i, h in units]
        k = [x.astype(F32) for x in k16]
        v = [xb_ref[rows_c[i], 2 * DN_WIDTH + h * DN_DIM:2 * DN_WIDTH + (h + 1) * DN_DIM].astype(F32)
             for i, h in units]

        b_col = [beta_c[i][:, h:h + 1] for i, h in units]
        g_col = [gc_c[i][:, DN_HEADS + h:DN_HEADS + h + 1] for i, h in units]
        g_row = [gc_t_c[i][DN_HEADS + h:DN_HEADS + h + 1, :] for i, h in units]
        g_last = [gcol[CHUNK - 1:CHUNK, :] for gcol in g_col]
        decay = [jnp.exp(jnp.where(tril, gcol - grow, neg_inf)) for gcol, grow in zip(g_col, g_row)]
        e_g = [jnp.exp(gcol) for gcol in g_col]
        kb = [kh * bh for kh, bh in zip(k, b_col)]
        a_mat = [jnp.where(tril_strict, _dot_nt(kbh.astype(BF16), kh16) * dh, 0.0)
                 for kbh, kh16, dh in zip(kb, k16, decay)]
        pw = [-a for a in a_mat]
        t_mat = [eye + p for p in pw]
        for _ in range(5):
            pw16 = [p.astype(BF16) for p in pw]
            pw = [_dot(p, p) for p in pw16]
            t_mat = [th + _dot(th.astype(BF16), p.astype(BF16)) for th, p in zip(t_mat, pw)]
        t16 = [th.astype(BF16) for th in t_mat]
        u = [_dot(th, (vh * bh).astype(BF16)) for th, vh, bh in zip(t16, v, b_col)]
        w = [_dot(th, (kbh * eh).astype(BF16)) for th, kbh, eh in zip(t16, kb, e_g)]
        attn = [jnp.where(tril, _dot_nt(qh.astype(BF16), kh16) * dh, 0.0).astype(BF16)
                for qh, kh16, dh in zip(q, k16, decay)]
        q_dec = [(qh * eh).astype(BF16) for qh, eh in zip(q, e_g)]
        k_tail = [(kh * jnp.exp(gl - gcol)).astype(BF16) for kh, gl, gcol in zip(k, g_last, g_col)]
        for n, (i, h) in enumerate(units):
            hc = slice(h * DN_DIM, (h + 1) * DN_DIM)
            u_scr[rows_c[i], hc] = u[n]
            w_scr[rows_c[i], hc] = w[n].astype(BF16)
            qd_scr[rows_c[i], hc] = q_dec[n]
            kt_scr[rows_c[i], hc] = k_tail[n]
            attn_scr[h, rows_c[i], :] = attn[n]
        return carry

    lax.fori_loop(0, n_chunks // chunks_per_iter, intra_body, 0)

    def scan_body(c, carry):
        r0 = pl.multiple_of(c * CHUNK, CHUNK)
        rows = pl.ds(r0, CHUNK)
        decay_last = jnp.exp(gc_scr[pl.ds(r0 + CHUNK - 1, 1), :])
        hcs = [slice(h * DN_DIM, (h + 1) * DN_DIM) for h in heads]
        state = [state_scr[h] for h in heads]
        u = [u_scr[rows, hc] for hc in hcs]
        w16 = [w_scr[rows, hc] for hc in hcs]
        qd = [qd_scr[rows, hc] for hc in hcs]
        kt = [kt_scr[rows, hc] for hc in hcs]
        attn = [attn_scr[h, rows, :] for h in heads]
        z = [z_ref[rows, hc] for hc in hcs]
        s16 = [sh.astype(BF16) for sh in state]
        v_new = [(uh - _dot(wh, sh)).astype(BF16) for uh, wh, sh in zip(u, w16, s16)]
        o = [_dot(qh, sh) + _dot(ah, vn) for qh, sh, ah, vn in zip(qd, s16, attn, v_new)]
        new_state = [sh * decay_last[:, DN_HEADS + h:DN_HEADS + h + 1] + _dot_tn(kth, vn)
                     for h, sh, kth, vn in zip(heads, state, kt, v_new)]
        o = [oh * lax.rsqrt(jnp.mean(oh * oh, axis=-1, keepdims=True) + EPS) * ognorm_ref[...] for oh in o]
        o = [(oh * _silu(zh.astype(F32))).astype(BF16) for oh, zh in zip(o, z)]
        for h in heads:
            state_scr[h] = new_state[h]
        for h in heads:
            o_ref[rows, hcs[h]] = o[h]
        return carry

    lax.fori_loop(0, n_chunks, scan_body, 0, unroll=4)


def _deltanet(qkvn3, proj3, ab3, a_log_l, dtb_l, og_norm, tile):
    b, s, _ = proj3.shape
    lcum = jnp.tril(jnp.ones((tile, tile), F32))
    same_chunk = (jnp.arange(tile)[:, None] // CHUNK) == (jnp.arange(tile)[None, :] // CHUNK)
    lcum = jnp.where(same_chunk, lcum, 0.0).astype(BF16)
    return pl.pallas_call(
        functools.partial(_deltanet_kernel, tile=tile),
        grid=(b, s // tile),
        in_specs=[
            pl.BlockSpec((None, tile, 3 * DN_WIDTH), lambda bi, t: (bi, t, 0)),
            pl.BlockSpec((None, tile, DN_WIDTH), lambda bi, t: (bi, t, COL_Z // DN_WIDTH)),
            pl.BlockSpec((None, tile, LANES), lambda bi, t: (bi, t, 0)),
            pl.BlockSpec((1, LANES), lambda bi, t: (0, 0)),
            pl.BlockSpec((1, LANES), lambda bi, t: (0, 0)),
            pl.BlockSpec((1, DN_DIM), lambda bi, t: (0, 0)),
            pl.BlockSpec((tile, tile), lambda bi, t: (0, 0)),
        ],
        out_specs=pl.BlockSpec((None, tile, DN_WIDTH), lambda bi, t: (bi, t, 0)),
        out_shape=jax.ShapeDtypeStruct((b, s, DN_WIDTH), BF16),
        scratch_shapes=[
            pltpu.VMEM((tile, LANES), F32),
            pltpu.VMEM((tile, LANES), F32),
            pltpu.VMEM((DN_HEADS, DN_DIM, DN_DIM), F32),
            pltpu.VMEM((tile, DN_WIDTH), F32),
            pltpu.VMEM((tile, DN_WIDTH), BF16),
            pltpu.VMEM((tile, DN_WIDTH), BF16),
            pltpu.VMEM((tile, DN_WIDTH), BF16),
            pltpu.VMEM((DN_HEADS, tile, CHUNK), BF16),
        ],
        compiler_params=_params(("parallel", "arbitrary")),
        name="deltanet",
    )(qkvn3, proj3, ab3, a_log_l, dtb_l, og_norm, lcum)


def _merge_kernel(o0_ref, o1_ref, o2_ref, l0_ref, l1_ref, l2_ref, ob_ref, ga_ref, gb_ref, x_ref,
                  wa_ref, wb_ref, wo_ref, hexp_ref, gain_ref, wr_ref, rbias_ref,
                  x1_ref, h2_ref, logit_ref, o1_scr, o2_scr, l1_scr, l2_scr, *, tm):
    for g, src, dst in ((1, o1_ref, o1_scr), (2, o2_ref, o2_scr), (1, l1_ref, l1_scr), (2, l2_ref, l2_scr)):
        dil = ATTN_GROUPS[g][1]
        for r in range(dil):
            for c in range(dst.shape[0]):
                dst[c, pl.ds(r, tm // dil, stride=dil), :] = src[r, :, c * LANES:(c + 1) * LANES].astype(F32)

    def slabs(scr):
        return jnp.concatenate([scr[c] for c in range(scr.shape[0])], axis=-1)

    l0, l1, l2 = l0_ref[...], l1_scr[0], l2_scr[0]
    m = jnp.maximum(jnp.maximum(l0, l1), l2)
    e0, e1, e2 = jnp.exp(l0 - m), jnp.exp(l1 - m), jnp.exp(l2 - m)
    tot = e0 + e1 + e2
    hexp = hexp_ref[...]
    o_a = jnp.zeros(o0_ref.shape, F32)
    for e, o_val in ((e0, o0_ref[...].astype(F32)), (e1, slabs(o1_scr)), (e2, slabs(o2_scr))):
        w_hi, w_lo = _split2(e / tot)
        o_a = o_a + _dot(jnp.concatenate([w_hi, w_lo], axis=1), hexp) * o_val
    ya = _dot(o_a.astype(BF16), wa_ref[...])
    yb = _dot(ob_ref[...], wb_ref[...])
    merged = _sigmoid(ga_ref[...].astype(F32)) * ya + _sigmoid(gb_ref[...].astype(F32)) * yb
    x1 = x_ref[...] + _dot(merged.astype(BF16), wo_ref[...])
    x1_ref[...] = x1
    h2 = x1 * lax.rsqrt(jnp.mean(x1 * x1, axis=-1, keepdims=True) + EPS) * gain_ref[...]
    hi, lo = _split2(h2)
    h2_ref[...] = hi
    hi_terms = _dot(hi, wr_ref[...])
    logit_ref[...] = (hi_terms[:, :LANES] + hi_terms[:, LANES:] + _dot(lo, wr_ref[:, :LANES])
                      + rbias_ref[...])


def _merge(o0, l0, o_perm, l_perm, o_b, proj, x2, wa, wb, wo, gain, wr, rbias, s, tm):
    t, d = x2.shape
    tps = s // tm
    head_of_col = jnp.arange(GROUP_WIDTH) // HEAD_DIM_A
    hexp = (jnp.arange(LANES)[:, None] == head_of_col[None, :]).astype(BF16)
    hexp = jnp.concatenate([hexp, hexp], axis=0)

    def row(wd, cblk=0):
        return pl.BlockSpec((tm, wd), lambda i: (i, cblk))

    def perm(arr):
        _, dil, _, wd = arr.shape
        return pl.BlockSpec((None, dil, tm // dil, wd), lambda i: (i // tps, 0, i % tps, 0))

    def full(a):
        return pl.BlockSpec(a.shape, lambda i: (0, 0))

    return pl.pallas_call(
        functools.partial(_merge_kernel, tm=tm),
        grid=(t // tm,),
        in_specs=[row(GROUP_WIDTH), perm(o_perm[0]), perm(o_perm[1]),
                  row(LANES), perm(l_perm[0]), perm(l_perm[1]),
                  row(DN_WIDTH), row(d, COL_GA // d), row(d, COL_GB // d), row(d),
                  full(wa), full(wb), full(wo), full(hexp), full(gain), full(wr),
                  full(rbias)],
        out_specs=[row(d), row(d), row(LANES)],
        out_shape=[
            jax.ShapeDtypeStruct((t, d), F32),
            jax.ShapeDtypeStruct((t, d), BF16),
            jax.ShapeDtypeStruct((t, LANES), F32),
        ],
        scratch_shapes=[pltpu.VMEM((GROUP_WIDTH // LANES, tm, LANES), F32),
                        pltpu.VMEM((GROUP_WIDTH // LANES, tm, LANES), F32),
                        pltpu.VMEM((1, tm, LANES), F32), pltpu.VMEM((1, tm, LANES), F32)],
        compiler_params=_params(("parallel",)),
        name="merge",
    )(o0, o_perm[0], o_perm[1], l0, l_perm[0], l_perm[1], o_b, proj, proj, x2,
      wa, wb, wo, hexp, gain, wr, rbias)


def _routing_weights(logits):
    lane_i = lax.broadcasted_iota(jnp.int32, logits.shape, 1)
    lane = lane_i.astype(F32)
    neg_inf = jnp.float32(-jnp.inf)
    big = jnp.float32(1 << 20)
    is_group = lane_i < N_EXPERT_GROUPS
    gl = jnp.where(is_group, logits, neg_inf)
    gmax = jnp.max(gl, axis=-1, keepdims=True)
    gsum = jnp.sum(jnp.exp(gl - gmax), axis=-1, keepdims=True)
    p_g = 1.0 / gsum
    g_idx = jnp.min(jnp.where(gl == gmax, lane, big), axis=-1, keepdims=True)
    n_exp = N_EXPERT_GROUPS * EXPERTS_PER_GROUP
    is_exp = jnp.logical_and(lane_i >= N_EXPERT_GROUPS, lane_i < N_EXPERT_GROUPS + n_exp)
    exp_group = jnp.right_shift(lane_i - N_EXPERT_GROUPS, 2).astype(F32)
    assert EXPERTS_PER_GROUP == 4
    sel = jnp.logical_and(is_exp, exp_group == g_idx)
    el = jnp.where(sel, logits, neg_inf)
    emax = jnp.max(el, axis=-1, keepdims=True)
    ee = jnp.exp(el - emax)
    esum = jnp.sum(ee, axis=-1, keepdims=True)
    idx1 = jnp.min(jnp.where(el == emax, lane, big), axis=-1, keepdims=True)
    el2 = jnp.where(lane == idx1, neg_inf, el)
    e2max = jnp.max(el2, axis=-1, keepdims=True)
    idx2 = jnp.min(jnp.where(el2 == e2max, lane, big), axis=-1, keepdims=True)
    p1 = 1.0 / esum
    p2 = jnp.exp(e2max - emax) / esum
    top_sum = p1 + p2
    w = jnp.where(lane == idx1, p1 / top_sum, jnp.where(lane == idx2, p2 / top_sum, 0.0))
    return w * p_g, g_idx


def _moe_kernel(h2_ref, logit_ref, x1_ref, wg_ref, wu_ref, wd_ref, lstrict_ref, out_ref,
                hs_scr, cws_scr, pos_scr, acc_scr, off_smem, *, tm, sub):
    g = pl.program_id(1)
    lane = lax.broadcasted_iota(jnp.int32, (tm, LANES), 1)

    @pl.when(g == 0)
    def _():
        cw, g_idx = _routing_weights(logit_ref[...])
        onehot = jnp.where(lane.astype(F32) == g_idx, 1.0, 0.0)
        earlier = _dot(lstrict_ref[...], onehot.astype(BF16))
        counts = jnp.sum(onehot, axis=0, keepdims=True)
        lane_row = lax.broadcasted_iota(jnp.int32, (1, LANES), 1)
        start = jnp.zeros((1, LANES), F32)
        for k in range(N_EXPERT_GROUPS - 1):
            start = start + jnp.where(lane_row > k, counts[:, k:k + 1], 0.0)
        pos = jnp.sum(onehot * (earlier + start), axis=-1, keepdims=True)
        pos_b = jnp.broadcast_to(pos, (tm, LANES))
        pos_scr[...] = pos_b
        pos_row = jnp.transpose(pos_b)[0:1, :]
        dest = lax.broadcasted_iota(jnp.int32, (tm, tm), 0).astype(F32)
        gather = jnp.where(dest == pos_row, 1.0, 0.0).astype(BF16)
        hs_scr[...] = _dot(gather, h2_ref[...]).astype(BF16)
        cw_sorted = _dot(gather, jnp.concatenate(_split2(cw), axis=1))
        cws_scr[...] = cw_sorted[:, :LANES] + cw_sorted[:, LANES:]
        for k in range(N_EXPERT_GROUPS):
            off_smem[k] = start[0, k].astype(jnp.int32)
        off_smem[N_EXPERT_GROUPS] = jnp.int32(tm)
        acc_scr[...] = jnp.zeros_like(acc_scr)

    first = off_smem[g]
    last = off_smem[g + 1]
    lane_s = lax.broadcasted_iota(jnp.int32, (sub, LANES), 1)
    for s in range(tm // sub):
        @pl.when(jnp.logical_and(first < (s + 1) * sub, last > s * sub))
        def _(s=s):
            rows = slice(s * sub, (s + 1) * sub)
            hs = hs_scr[rows, :]
            hg = _dot(hs, wg_ref[...])
            hu = _dot(hs, wu_ref[...])
            cw = cws_scr[rows, :]
            parts = []
            for j in range(EXPERTS_PER_GROUP):
                cols = slice(j * D_EXPERT, (j + 1) * D_EXPERT)
                want = N_EXPERT_GROUPS + g * EXPERTS_PER_GROUP + j
                c_col = jnp.sum(jnp.where(lane_s == want, cw, 0.0), axis=-1, keepdims=True)
                parts.append((_silu(hg[:, cols]) * hu[:, cols] * c_col).astype(BF16))
            act = jnp.concatenate(parts, axis=-1)
            acc_scr[rows, :] += _dot(act, wd_ref[...])

    @pl.when(g == N_EXPERT_GROUPS - 1)
    def _():
        src = lax.broadcasted_iota(jnp.int32, (tm, tm), 1).astype(F32)
        scatter = jnp.where(pos_scr[:, 0:1] == src, 1.0, 0.0).astype(BF16)
        out_ref[...] = x1_ref[...] + _dot(scatter, acc_scr[...].astype(BF16))


def _moe(h2, logits, x1, wg, wu, wd, tm, sub):
    t, d = x1.shape
    gw = EXPERTS_PER_GROUP * D_EXPERT
    lstrict = jnp.tril(jnp.ones((tm, tm), F32), -1).astype(BF16)
    return pl.pallas_call(
        functools.partial(_moe_kernel, tm=tm, sub=sub),
        grid=(t // tm, N_EXPERT_GROUPS),
        in_specs=[
            pl.BlockSpec((tm, d), lambda i, g: (i, 0)),
            pl.BlockSpec((tm, LANES), lambda i, g: (i, 0)),
            pl.BlockSpec((tm, d), lambda i, g: (i, 0)),
            pl.BlockSpec((None, d, gw), lambda i, g: (g, 0, 0)),
            pl.BlockSpec((None, d, gw), lambda i, g: (g, 0, 0)),
            pl.BlockSpec((None, gw, d), lambda i, g: (g, 0, 0)),
            pl.BlockSpec((tm, tm), lambda i, g: (0, 0)),
        ],
        out_specs=pl.BlockSpec((tm, d), lambda i, g: (i, 0)),
        out_shape=jax.ShapeDtypeStruct((t, d), F32),
        scratch_shapes=[pltpu.VMEM((tm, d), BF16), pltpu.VMEM((tm, LANES), F32),
                        pltpu.VMEM((tm, LANES), F32), pltpu.VMEM((tm, d), F32),
                        pltpu.SMEM((8,), jnp.int32)],
        compiler_params=_params(("parallel", "arbitrary")),
        name="moe",
    )(h2, logits, x1, wg, wu, wd, lstrict)


def _ple_kernel(x_ref, p_ref, gain_ref, wpg_ref, wple_ref, out_ref):
    x = x_ref[...]
    h3 = x * lax.rsqrt(jnp.mean(x * x, axis=-1, keepdims=True) + EPS) * gain_ref[...]
    gate = _sigmoid(_dot(h3.astype(BF16), wpg_ref[...]))
    ple = _dot(p_ref[...].astype(BF16), wple_ref[...])
    out_ref[...] = x + gate * ple


def _ple(x2, p2, gain, wpg, wple, tm):
    t, d = x2.shape
    pd = p2.shape[1]
    return pl.pallas_call(
        _ple_kernel,
        grid=(t // tm,),
        in_specs=[
            pl.BlockSpec((tm, d), lambda i: (i, 0)),
            pl.BlockSpec((tm, pd), lambda i: (i, 0)),
            pl.BlockSpec((1, d), lambda i: (0, 0)),
            pl.BlockSpec((d, d), lambda i: (0, 0)),
            pl.BlockSpec((pd, d), lambda i: (0, 0)),
        ],
        out_specs=pl.BlockSpec((tm, d), lambda i: (i, 0)),
        out_shape=jax.ShapeDtypeStruct((t, d), F32),
        compiler_params=_params(("parallel",)),
        name="ple",
    )(x2, p2, gain, wpg, wple)


def _lane_pad(v, offset):
    out = jnp.zeros((1, LANES), F32)
    return out.at[0, offset:offset + v.shape[0]].set(v.astype(F32))


def _layer(x2, p2, b, s, norm_mix, w_in, q_norm, k_norm, conv_w, a_log, dt_bias, dn_out_norm,
           w_branch_a, w_branch_b, w_out, norm_ffn, w_router_group, b_router_group,
           w_router_expert, b_router_expert, w_expert_gate, w_expert_up, w_expert_down,
           norm_ple, w_ple, w_ple_gate):
    t, d = x2.shape
    tiles = _tiles(s)
    qa = N_GROUPS * GROUP_WIDTH
    o_b0 = 3 * qa
    o_z = o_b0 + 3 * DN_WIDTH
    o_beta = o_z + DN_WIDTH
    o_alpha = o_beta + DN_HEADS
    o_ga = o_alpha + DN_HEADS
    o_gb = o_ga + d
    w_in = w_in.astype(F32)
    gw_a = GROUP_WIDTH
    w_groups = [jnp.concatenate([w_in[:, part * qa + g * gw_a:part * qa + (g + 1) * gw_a]
                                 for part in range(3)], axis=1).astype(BF16) for g in range(N_GROUPS)]
    w_dn = w_in[:, o_b0:o_z].astype(BF16)
    w_plain = jnp.concatenate([
        w_in[:, o_z:o_beta],
        w_in[:, o_ga:o_gb],
        w_in[:, o_gb:o_gb + d],
    ], axis=1).astype(BF16)
    w_ab = jnp.zeros((d, LANES), F32).at[:, 0:2 * DN_HEADS].set(w_in[:, o_beta:o_ga])
    wab = jnp.concatenate(_split2(w_ab), axis=1)
    qg = jnp.tile(q_norm.astype(F32), HEADS_A) * (HEAD_DIM_A ** -0.5 * LOG2E)
    kg = jnp.tile(k_norm.astype(F32), HEADS_A)
    qkgain = jnp.concatenate([qg, kg])[None, :]

    proj, ab, h, qkv = _in_projection(x2, norm_mix.astype(F32)[None, :], w_plain, w_groups, wab,
                                      qkgain, b, s, tm=tiles["in_proj"])
    proj3 = proj.reshape(b, s, PROJ_WIDTH)
    qkv_dn = _in_projection_dn(h, w_dn, conv_w.astype(F32), s, tm=tiles["in_proj"])

    (o0, l0), (o1, l1), (o2, l2) = [_attention_group(a, 0, 1, 2, tq=tiles["attn"]) for a in qkv]

    o_b = _deltanet(qkv_dn.reshape(b, s, 3 * DN_WIDTH), proj3, ab.reshape(b, s, LANES),
                    _lane_pad(a_log, DN_HEADS), _lane_pad(dt_bias, DN_HEADS),
                    dn_out_norm.astype(F32)[None, :], tile=tiles["deltanet"]).reshape(t, DN_WIDTH)

    w_r = jnp.zeros((d, LANES), F32)
    w_r = w_r.at[:, 0:N_EXPERT_GROUPS].set(w_router_group.astype(F32))
    n_exp = N_EXPERT_GROUPS * EXPERTS_PER_GROUP
    w_r = w_r.at[:, N_EXPERT_GROUPS:N_EXPERT_GROUPS + n_exp].set(w_router_expert.astype(F32))
    w_r2 = jnp.concatenate(_split2(w_r), axis=1)
    rbias = _lane_pad(jnp.concatenate([b_router_group.astype(F32), b_router_expert.astype(F32)]), 0)

    x1, h2, logits = _merge(o0.reshape(t, GROUP_WIDTH), l0.reshape(t, LANES), (o1, o2), (l1, l2), o_b,
                            proj, x2, w_branch_a.astype(BF16), w_branch_b.astype(BF16),
                            w_out.astype(BF16), norm_ffn.astype(F32)[None, :], w_r2, rbias,
                            s, tm=tiles["merge"])

    gw = EXPERTS_PER_GROUP * D_EXPERT
    wg = jnp.transpose(w_expert_gate, (0, 2, 1, 3)).reshape(N_EXPERT_GROUPS, d, gw).astype(BF16)
    wu = jnp.transpose(w_expert_up, (0, 2, 1, 3)).reshape(N_EXPERT_GROUPS, d, gw).astype(BF16)
    wd = w_expert_down.reshape(N_EXPERT_GROUPS, gw, d).astype(BF16)
    x2n = _moe(h2, logits, x1, wg, wu, wd, tm=tiles["moe"], sub=tiles["moe_slab"])

    return _ple(x2n, p2, norm_ple.astype(F32)[None, :], w_ple_gate.astype(BF16), w_ple.astype(BF16),
                tm=tiles["ple"])


def kernel(x, p, norm_mix, w_in, q_norm, k_norm, conv_w, a_log, dt_bias, dn_out_norm, w_branch_a, w_branch_b, w_out, norm_ffn, w_router_group, b_router_group, w_router_expert, b_router_expert, w_expert_gate, w_expert_up, w_expert_down, norm_ple, w_ple, w_ple_gate):
    b, s, d = x.shape
    depth = w_in.shape[0]
    x2 = x.astype(F32).reshape(b * s, d)
    for i in range(depth):
        x2 = _layer(x2, p[i].reshape(b * s, -1), b, s, norm_mix[i], w_in[i], q_norm[i], k_norm[i],
                    conv_w[i], a_log[i], dt_bias[i], dn_out_norm[i], w_branch_a[i], w_branch_b[i],
                    w_out[i], norm_ffn[i], w_router_group[i], b_router_group[i], w_router_expert[i],
                    b_router_expert[i], w_expert_gate[i], w_expert_up[i], w_expert_down[i],
                    norm_ple[i], w_ple[i], w_ple_gate[i])
    return x2.reshape(b, s, d)
```
